```python
import jax, jax.numpy as jnp
from jax import lax
import numpy as np

D_MODEL = 2048
BATCH = 2
SEQ = 8192
DEPTH = 1

POOL_WINDOWS = (2, 4, 8, 16)
POOL_GROUPS = len(POOL_WINDOWS)
POOL_GROUP_WIDTH = D_MODEL // 8
POOL_WIDTH = POOL_GROUPS * POOL_GROUP_WIDTH
LRU_WIDTH = D_MODEL
LRU_BLOCK_WIDTH = 256
LRU_BLOCKS = LRU_WIDTH // LRU_BLOCK_WIDTH
LRU_CONV_WIDTH = 4
LRU_C = 8.0
LRU_A_MIN = 0.9
LRU_A_MAX = 0.999
N_BRANCHES = 2
IN_WIDTH = POOL_WIDTH + 2 * LRU_WIDTH + N_BRANCHES * D_MODEL
D_FF = 3 * D_MODEL
FFN_CONV_WIDTH = 3
EPS = 1e-6

kernel_name = "hybrid_pool_rglru_gated_block"


def rms_norm(x, g):
    xf = x.astype(jnp.float32)
    y = xf * lax.rsqrt(jnp.mean(xf * xf, axis=-1, keepdims=True) + EPS)
    return (y * g.astype(jnp.float32)).astype(x.dtype)


def causal_depthwise_conv(x, w, b):
    K = w.shape[0]
    S = x.shape[1]
    xp = jnp.pad(x, ((0, 0), (K - 1, 0), (0, 0)))
    out = b
    for k in range(K):
        out = out + xp[:, k:k + S] * w[k]
    return out


def pool_mixer(u, w_pool, pool_scale):
    B, S, _ = u.shape
    uf = u.astype(jnp.float32)
    c = jnp.cumsum(uf, axis=1)
    pos = jnp.arange(1, S + 1, dtype=jnp.float32)
    means = []
    for g, w in enumerate(POOL_WINDOWS):
        cg = c[..., g * POOL_GROUP_WIDTH:(g + 1) * POOL_GROUP_WIDTH]
        shifted = jnp.pad(cg, ((0, 0), (w, 0), (0, 0)))[:, :S]
        count = jnp.minimum(pos, float(w))[None, :, None]
        means.append((cg - shifted) / count)
    mean = jnp.stack(means, axis=2)
    d = (mean - uf.reshape(B, S, POOL_GROUPS, POOL_GROUP_WIDTH)).astype(u.dtype)
    y = jnp.einsum('bsgc,gcd->bsgd', d, w_pool).reshape(B, S, POOL_WIDTH)
    return y * pool_scale


def rg_lru(x, w_a, b_a, w_i, b_i, lam):
    B, S, R = x.shape
    xb = x.reshape(B, S, LRU_BLOCKS, LRU_BLOCK_WIDTH)
    r = jax.nn.sigmoid(jnp.einsum('bshc,hcd->bshd', xb, w_a).reshape(B, S, R) + b_a)
    i = jax.nn.sigmoid(jnp.einsum('bshc,hcd->bshd', xb, w_i).reshape(B, S, R) + b_i)
    log_a = -LRU_C * r.astype(jnp.float32) * jax.nn.softplus(-lam.astype(jnp.float32))
    a = jnp.exp(log_a)
    mult = jnp.sqrt(-jnp.expm1(2.0 * log_a))
    bx = mult * (i * x).astype(jnp.float32)

    def combine(left, right):
        a1, b1 = left
        a2, b2 = right
        return a1 * a2, a2 * b1 + b2

    _, h = lax.associative_scan(combine, (a, bx), axis=1)
    return h.astype(x.dtype)


def setup_inputs(seed: int = 0) -> dict:
    key = jax.random.key(seed)
    ks = jax.random.split(key, 24)
    f32 = jnp.float32

    def nrm(k, shape, fan_in):
        return jax.random.normal(k, shape, f32) * (fan_in ** -0.5)

    def gain(k, shape):
        return 1.0 + 0.02 * jax.random.normal(k, shape, f32)

    def bias(k, shape):
        return 0.01 * jax.random.normal(k, shape, f32)

    L = DEPTH
    u = jax.random.uniform(ks[12], (L, LRU_WIDTH), f32, LRU_A_MIN, LRU_A_MAX)
    s = u ** (1.0 / LRU_C)
    lru_lambda = jnp.log(s) - jnp.log1p(-s)
    return {
        "x": jax.random.normal(ks[0], (BATCH, SEQ, D_MODEL), f32),
        "g_mix": gain(ks[1], (L, D_MODEL)),
        "w_in": nrm(ks[2], (L, D_MODEL, IN_WIDTH), D_MODEL),
        "b_gate": bias(ks[3], (L, N_BRANCHES * D_MODEL)),
        "w_pool": nrm(ks[4], (L, POOL_GROUPS, POOL_GROUP_WIDTH, POOL_GROUP_WIDTH), POOL_GROUP_WIDTH),
        "pool_scale": gain(ks[5], (L, POOL_WIDTH)),
        "lru_conv_w": nrm(ks[6], (L, LRU_CONV_WIDTH, LRU_WIDTH), LRU_CONV_WIDTH),
        "lru_conv_b": bias(ks[7], (L, LRU_WIDTH)),
        "w_a": nrm(ks[8], (L, LRU_BLOCKS, LRU_BLOCK_WIDTH, LRU_BLOCK_WIDTH), LRU_BLOCK_WIDTH),
        "b_a": bias(ks[9], (L, LRU_WIDTH)),
        "w_i": nrm(ks[10], (L, LRU_BLOCKS, LRU_BLOCK_WIDTH, LRU_BLOCK_WIDTH), LRU_BLOCK_WIDTH),
        "b_i": bias(ks[11], (L, LRU_WIDTH)),
        "lru_lambda": lru_lambda,
        "w_pool_proj": nrm(ks[13], (L, POOL_WIDTH, D_MODEL), POOL_WIDTH),
        "w_lru_proj": nrm(ks[14], (L, LRU_WIDTH, D_MODEL), LRU_WIDTH),
        "w_out": nrm(ks[15], (L, D_MODEL, D_MODEL), D_MODEL),
        "g_mlp": gain(ks[16], (L, D_MODEL)),
        "w_up": nrm(ks[17], (L, D_MODEL, 2 * D_FF), D_MODEL),
        "ffn_conv_w": nrm(ks[18], (L, FFN_CONV_WIDTH, D_FF), FFN_CONV_WIDTH),
        "ffn_conv_b": bias(ks[19], (L, D_FF)),
        "w_down": nrm(ks[20], (L, D_FF, D_MODEL), D_FF),
        "g_final": gain(ks[21], (D_MODEL,)),
    }


def reference(x, g_mix, w_in, b_gate, w_pool, pool_scale, lru_conv_w, lru_conv_b,
              w_a, b_a, w_i, b_i, lru_lambda, w_pool_proj, w_lru_proj, w_out,
              g_mlp, w_up, ffn_conv_w, ffn_conv_b, w_down, g_final):
    B, S, D = x.shape
    for l in range(DEPTH):
        h = rms_norm(x, g_mix[l])
        proj = h @ w_in[l]
        p0 = POOL_WIDTH
        p1 = p0 + LRU_WIDTH
        p2 = p1 + LRU_WIDTH
        u_pool = proj[..., :p0]
        u_lru = proj[..., p0:p1]
        u_gelu = proj[..., p1:p2]
        gates = jax.nn.sigmoid(proj[..., p2:] + b_gate[l]).reshape(B, S, N_BRANCHES, D)

        y_pool = pool_mixer(u_pool, w_pool[l], pool_scale[l])
        v = causal_depthwise_conv(u_lru, lru_conv_w[l], lru_conv_b[l])
        y_lru = rg_lru(v, w_a[l], b_a[l], w_i[l], b_i[l], lru_lambda[l]) * jax.nn.gelu(u_gelu)

        merged = (gates[:, :, 0] * (y_pool @ w_pool_proj[l])
                  + gates[:, :, 1] * (y_lru @ w_lru_proj[l]))
        x = x + merged @ w_out[l]

        h2 = rms_norm(x, g_mlp[l])
        up = h2 @ w_up[l]
        gate_pre = up[..., :D_FF]
        val = up[..., D_FF:]
        gate = jax.nn.gelu(causal_depthwise_conv(gate_pre, ffn_conv_w[l], ffn_conv_b[l]))
        x = x + (gate * val) @ w_down[l]
    return rms_norm(x, g_final)
```

```python
import functools

import jax
import jax.numpy as jnp
from jax import lax
from jax.experimental import pallas as pl
from jax.experimental.pallas import tpu as pltpu

F32 = jnp.float32
BF16 = jnp.bfloat16

POOL_WINDOWS = (2, 4, 8, 16)
POOL_GROUP_WIDTH = 256
LRU_BLOCK_WIDTH = 256
LRU_C = 8.0
EPS = 1e-6

SUBLANES = 8
POOL_HALO = 16
CONV_HALO = SUBLANES
VMEM_LIMIT_BYTES = 56 * 1024 * 1024

INPROJ_TM = 512
INPROJ_TN = 1024
MIXER_T = 512
MERGE_TM = 256
MERGE_CHUNK = 512
FFN_T = 512
FFN_TF = 512


def _rms_norm(x, g):
    ms = jnp.mean(x * x, axis=-1, keepdims=True)
    return x * lax.rsqrt(ms + EPS) * g


def _dot(a, b):
    return jnp.dot(a, b, preferred_element_type=F32)


def _resident(shape):
    zeros = (0,) * len(shape)
    return pl.BlockSpec(shape, lambda *_: zeros, pipeline_mode=pl.Buffered(1))


def _inproj_kernel(n_mix, n_gelu, x_ref, g_ref, w_ref, b_ref,
                   mix_ref, gelu_ref, gates_ref, h_ref):
    j = pl.program_id(1)

    @pl.when(j == 0)
    def _():
        h_ref[...] = _rms_norm(x_ref[...], g_ref[...]).astype(BF16)

    p = _dot(h_ref[...], w_ref[...])

    @pl.when(j < n_mix)
    def _():
        mix_ref[...] = p

    @pl.when(jnp.logical_and(j >= n_mix, j < n_mix + n_gelu))
    def _():
        gelu_ref[...] = jax.nn.gelu(p).astype(BF16)

    @pl.when(j >= n_mix + n_gelu)
    def _():
        gates_ref[...] = jax.nn.sigmoid(p + b_ref[...]).astype(BF16)


def _inproj(x, g, w_in, b_gate, mix_width, gelu_width):
    n, d = x.shape
    in_width = w_in.shape[1]
    gate_width = in_width - mix_width - gelu_width
    tm, tn = INPROJ_TM, INPROJ_TN
    n_mix, n_gelu, n_gate = mix_width // tn, gelu_width // tn, gate_width // tn
    assert n % tm == 0 and n_mix * tn == mix_width and n_gelu * tn == gelu_width
    assert n_gate * tn == gate_width

    def clamp(j, lo, count):
        return jnp.clip(j - lo, 0, count - 1)

    return pl.pallas_call(
        functools.partial(_inproj_kernel, n_mix, n_gelu),
        grid=(n // tm, in_width // tn),
        in_specs=[
            pl.BlockSpec((tm, d), lambda i, j: (i, 0)),
            pl.BlockSpec((1, d), lambda i, j: (0, 0)),
            pl.BlockSpec((d, tn), lambda i, j: (0, j)),
            pl.BlockSpec((1, tn), lambda i, j: (0, clamp(j, n_mix + n_gelu, n_gate))),
        ],
        out_specs=[
            pl.BlockSpec((tm, tn), lambda i, j: (i, clamp(j, 0, n_mix))),
            pl.BlockSpec((tm, tn), lambda i, j: (i, clamp(j, n_mix, n_gelu))),
            pl.BlockSpec((tm, tn), lambda i, j: (i, clamp(j, n_mix + n_gelu, n_gate))),
        ],
        out_shape=[
            jax.ShapeDtypeStruct((n, mix_width), F32),
            jax.ShapeDtypeStruct((n, gelu_width), BF16),
            jax.ShapeDtypeStruct((n, gate_width), BF16),
        ],
        scratch_shapes=[pltpu.VMEM((tm, d), BF16)],
        compiler_params=pltpu.CompilerParams(
            dimension_semantics=("arbitrary", "arbitrary"),
            vmem_limit_bytes=VMEM_LIMIT_BYTES),
    )(x, g, w_in, b_gate)


def _mixer_kernel(pool_width, mix_ref, gelu_ref, wpool_ref, pscale_ref, convw_ref, convb_ref,
                  wa_ref, ba_ref, wi_ref, bi_ref, lam_ref,
                  ypool_ref, ylru_ref,
                  pext_ref, lext_ref, a_ref, bx_ref, h_ref, hcarry_ref):
    i = pl.program_id(1)
    t_rows = ypool_ref.shape[0]
    lru_width = ylru_ref.shape[1]

    @pl.when(i == 0)
    def _():
        pext_ref[0:POOL_HALO, :] = jnp.zeros((POOL_HALO, pool_width), F32)
        lext_ref[0:CONV_HALO, :] = jnp.zeros((CONV_HALO, lru_width), F32)
        hcarry_ref[...] = jnp.zeros_like(hcarry_ref)

    pext_ref[POOL_HALO:POOL_HALO + t_rows, :] = mix_ref[:, 0:pool_width]
    lext_ref[CONV_HALO:CONV_HALO + t_rows, :] = mix_ref[:, pool_width:]

    pos = (i * t_rows + 1 + lax.broadcasted_iota(jnp.int32, (t_rows, 1), 0)).astype(F32)
    for g, w in enumerate(POOL_WINDOWS):
        cols = slice(g * POOL_GROUP_WIDTH, (g + 1) * POOL_GROUP_WIDTH)
        e = pext_ref[:, cols]
        s = e
        k = 1
        while k < w:
            s = s + pltpu.roll(s, k, 0)
            k *= 2
        inv_count = 1.0 / jnp.minimum(pos, float(w))
        d = s[POOL_HALO:, :] * inv_count - e[POOL_HALO:, :]
        y = _dot(d.astype(BF16), wpool_ref[g]) * pscale_ref[:, cols]
        ypool_ref[:, cols] = y.astype(BF16)

    softplus_neg_lam = jax.nn.softplus(-lam_ref[...])
    n_taps = convw_ref.shape[0]
    for hd in range(lru_width // LRU_BLOCK_WIDTH):
        cols = slice(hd * LRU_BLOCK_WIDTH, (hd + 1) * LRU_BLOCK_WIDTH)
        e = lext_ref[:, cols]
        cw = convw_ref[:, cols]
        v = convb_ref[:, cols] + e * cw[n_taps - 1:n_taps, :]
        for k in range(1, n_taps):
            v = v + pltpu.roll(e, k, 0) * cw[n_taps - 1 - k:n_taps - k, :]
        v = v[CONV_HALO:, :]
        vb = v.astype(BF16)
        r = jax.nn.sigmoid(_dot(vb, wa_ref[hd]) + ba_ref[:, cols])
        ig = jax.nn.sigmoid(_dot(vb, wi_ref[hd]) + bi_ref[:, cols])
        log_a = (-LRU_C) * r * softplus_neg_lam[:, cols]
        a = jnp.exp(log_a)
        mult = jnp.sqrt(-jnp.tanh(log_a) * (a * a + 1.0))
        a_ref[:, cols] = a
        bx_ref[:, cols] = mult * (ig * v)

    def step(t, h):
        h = a_ref[pl.ds(t, 1), :] * h + bx_ref[pl.ds(t, 1), :]
        h_ref[pl.ds(t, 1), :] = h
        return h

    hcarry_ref[...] = lax.fori_loop(0, t_rows, step, hcarry_ref[...], unroll=8)
    ylru_ref[...] = (h_ref[...] * gelu_ref[...].astype(F32)).astype(BF16)

    pext_ref[0:POOL_HALO, :] = pext_ref[t_rows:t_rows + POOL_HALO, :]
    lext_ref[0:CONV_HALO, :] = lext_ref[t_rows:t_rows + CONV_HALO, :]


def _mixer(mix, act_gelu, w_pool, pool_scale, conv_w, conv_b, w_a, b_a, w_i, b_i, lam,
           batch, seq):
    n = mix.shape[0]
    pool_width = pool_scale.shape[1]
    lru_width = lam.shape[1]
    t = MIXER_T
    assert seq % t == 0
    nblk = seq // t
    row = lambda b, i: (b * nblk + i, 0)
    return pl.pallas_call(
        functools.partial(_mixer_kernel, pool_width),
        grid=(batch, nblk),
        in_specs=[
            pl.BlockSpec((t, pool_width + lru_width), row),
            pl.BlockSpec((t, lru_width), row),
            _resident(w_pool.shape), _resident(pool_scale.shape),
            _resident(conv_w.shape), _resident(conv_b.shape),
            _resident(w_a.shape), _resident(b_a.shape),
            _resident(w_i.shape), _resident(b_i.shape), _resident(lam.shape),
        ],
        out_specs=[pl.BlockSpec((t, pool_width), row), pl.BlockSpec((t, lru_width), row)],
        out_shape=[jax.ShapeDtypeStruct((n, pool_width), BF16),
                   jax.ShapeDtypeStruct((n, lru_width), BF16)],
        scratch_shapes=[
            pltpu.VMEM((POOL_HALO + t, pool_width), F32),
            pltpu.VMEM((CONV_HALO + t, lru_width), F32),
            pltpu.VMEM((t, lru_width), F32),
            pltpu.VMEM((t, lru_width), F32),
            pltpu.VMEM((t, lru_width), F32),
            pltpu.VMEM((1, lru_width), F32),
        ],
        compiler_params=pltpu.CompilerParams(
            dimension_semantics=("arbitrary", "arbitrary"),
            vmem_limit_bytes=VMEM_LIMIT_BYTES),
    )(mix, act_gelu, w_pool, pool_scale, conv_w, conv_b, w_a, b_a, w_i, b_i, lam)


def _merge_kernel(yp_ref, yl_ref, gates_ref, x_ref, wpp_ref, wlp_ref, wout_ref, o_ref, m_ref):
    d = o_ref.shape[1]
    ch = MERGE_CHUNK
    for c in range(d // ch):
        cols = slice(c * ch, (c + 1) * ch)
        p = _dot(yp_ref[...], wpp_ref[:, cols])
        q = _dot(yl_ref[...], wlp_ref[:, cols])
        g_pool = gates_ref[:, cols].astype(F32)
        g_lru = gates_ref[:, d + c * ch:d + (c + 1) * ch].astype(F32)
        m_ref[:, cols] = (g_pool * p + g_lru * q).astype(BF16)
    for c in range(d // ch):
        cols = slice(c * ch, (c + 1) * ch)
        o_ref[:, cols] = x_ref[:, cols] + _dot(m_ref[...], wout_ref[:, cols])


def _merge(y_pool, y_lru, gates, x, w_pool_proj, w_lru_proj, w_out):
    n, d = x.shape
    tm = MERGE_TM
    assert n % tm == 0 and d % MERGE_CHUNK == 0
    row = lambda i: (i, 0)
    return pl.pallas_call(
        _merge_kernel,
        grid=(n // tm,),
        in_specs=[
            pl.BlockSpec((tm, y_pool.shape[1]), row),
            pl.BlockSpec((tm, y_lru.shape[1]), row),
            pl.BlockSpec((tm, gates.shape[1]), row),
            pl.BlockSpec((tm, d), row),
            _resident(w_pool_proj.shape), _resident(w_lru_proj.shape), _resident(w_out.shape),
        ],
        out_specs=pl.BlockSpec((tm, d), row),
        out_shape=jax.ShapeDtypeStruct((n, d), F32),
        scratch_shapes=[pltpu.VMEM((tm, d), BF16)],
        compiler_params=pltpu.CompilerParams(
            dimension_semantics=("arbitrary",),
            vmem_limit_bytes=VMEM_LIMIT_BYTES),
    )(y_pool, y_lru, gates, x, w_pool_proj, w_lru_proj, w_out)


def _ffn_kernel(apply_final, x_ref, g_ref, wg_ref, wv_ref, cw_ref, cb_ref, wd_ref, gfin_ref,
                o_ref, h2_ref, acc_ref, ext_ref, carry_ref):
    i = pl.program_id(1)
    f = pl.program_id(2)
    t_rows = x_ref.shape[0]

    @pl.when(f == 0)
    def _():
        h2_ref[...] = _rms_norm(x_ref[...], g_ref[...]).astype(BF16)
        acc_ref[...] = jnp.zeros_like(acc_ref)

    gate_pre = _dot(h2_ref[...], wg_ref[...])
    val = _dot(h2_ref[...], wv_ref[...])

    @pl.when(i == 0)
    def _():
        ext_ref[0:CONV_HALO, :] = jnp.zeros((CONV_HALO, ext_ref.shape[1]), F32)

    @pl.when(i > 0)
    def _():
        ext_ref[0:CONV_HALO, :] = carry_ref[f]

    ext_ref[CONV_HALO:, :] = gate_pre
    carry_ref[f] = gate_pre[t_rows - CONV_HALO:, :]

    e = ext_ref[...]
    n_taps = cw_ref.shape[0]
    conv = cb_ref[...] + e * cw_ref[n_taps - 1:n_taps, :]
    for k in range(1, n_taps):
        conv = conv + pltpu.roll(e, k, 0) * cw_ref[n_taps - 1 - k:n_taps - k, :]
    act = (jax.nn.gelu(conv[CONV_HALO:, :]) * val).astype(BF16)
    acc_ref[...] += _dot(act, wd_ref[...])

    @pl.when(f == pl.num_programs(2) - 1)
    def _():
        y = x_ref[...] + acc_ref[...]
        if apply_final:
            y = _rms_norm(y, gfin_ref[...])
        o_ref[...] = y


def _ffn(x, g, w_up, conv_w, conv_b, w_down, g_final, apply_final, batch, seq):
    n, d = x.shape
    d_ff = w_down.shape[0]
    t, tf = FFN_T, FFN_TF
    assert seq % t == 0 and d_ff % tf == 0
    nblk, nf = seq // t, d_ff // tf
    row = lambda b, i, f: (b * nblk + i, 0)
    return pl.pallas_call(
        functools.partial(_ffn_kernel, apply_final),
        grid=(batch, nblk, nf),
        in_specs=[
            pl.BlockSpec((t, d), row),
            pl.BlockSpec((1, d), lambda b, i, f: (0, 0)),
            pl.BlockSpec((d, tf), lambda b, i, f: (0, f)),
            pl.BlockSpec((d, tf), lambda b, i, f: (0, nf + f)),
            pl.BlockSpec((conv_w.shape[0], tf), lambda b, i, f: (0, f)),
            pl.BlockSpec((1, tf), lambda b, i, f: (0, f)),
            pl.BlockSpec((tf, d), lambda b, i, f: (f, 0)),
            pl.BlockSpec((1, d), lambda b, i, f: (0, 0)),
        ],
        out_specs=pl.BlockSpec((t, d), row),
        out_shape=jax.ShapeDtypeStruct((n, d), F32),
        scratch_shapes=[
            pltpu.VMEM((t, d), BF16),
            pltpu.VMEM((t, d), F32),
            pltpu.VMEM((CONV_HALO + t, tf), F32),
            pltpu.VMEM((nf, CONV_HALO, tf), F32),
        ],
        compiler_params=pltpu.CompilerParams(
            dimension_semantics=("arbitrary", "arbitrary", "arbitrary"),
            vmem_limit_bytes=VMEM_LIMIT_BYTES),
    )(x, g, w_up, w_up, conv_w, conv_b, w_down, g_final)


def kernel(x, g_mix, w_in, b_gate, w_pool, pool_scale, lru_conv_w, lru_conv_b, w_a, b_a, w_i,
           b_i, lru_lambda, w_pool_proj, w_lru_proj, w_out, g_mlp, w_up, ffn_conv_w, ffn_conv_b,
           w_down, g_final):
    batch, seq, d = x.shape
    depth = w_in.shape[0]
    pool_width = pool_scale.shape[1]
    lru_width = lru_lambda.shape[1]
    row2 = lambda v: v.reshape(1, -1)
    xf = x.reshape(batch * seq, d)
    for l in range(depth):
        mix, act_gelu, gates = _inproj(
            xf, row2(g_mix[l]), w_in[l].astype(BF16), row2(b_gate[l]),
            pool_width + lru_width, lru_width)
        y_pool, y_lru = _mixer(
            mix, act_gelu, w_pool[l].astype(BF16), row2(pool_scale[l]),
            lru_conv_w[l], row2(lru_conv_b[l]), w_a[l].astype(BF16), row2(b_a[l]),
            w_i[l].astype(BF16), row2(b_i[l]), row2(lru_lambda[l]), batch, seq)
        xf = _merge(y_pool, y_lru, gates, xf, w_pool_proj[l].astype(BF16),
                    w_lru_proj[l].astype(BF16), w_out[l].astype(BF16))
        xf = _ffn(xf, row2(g_mlp[l]), w_up[l].astype(BF16), ffn_conv_w[l], row2(ffn_conv_b[l]),
                  w_down[l].astype(BF16), row2(g_final), l == depth - 1, batch, seq)
    return xf.reshape(batch, seq, d)
```

```python
import functools

import jax
import jax.numpy as jnp
from jax import lax
from jax.experimental import pallas as pl
from jax.experimental.pallas import tpu as pltpu

F32 = jnp.float32
BF16 = jnp.bfloat16

POOL_WINDOWS = (2, 4, 8, 16)
POOL_GROUP_WIDTH = 256
LRU_BLOCK_WIDTH = 256
LRU_C = 8.0
EPS = 1e-6

SUBLANES = 8
POOL_SUB = 4
POOL_SUB_HALO = 16
POOL_HALO = 24
CONV_HALO = SUBLANES
VMEM_LIMIT_BYTES = 56 * 1024 * 1024

ROW_SUB = 256
INPROJ_TM = 1024
INPROJ_TN = 1024
MIXER_T = 512
MERGE_TM = 256
MERGE_CHUNK = 512
FFN_T = 512
FFN_TF = 512


def _rms_norm(x, g):
    ms = jnp.mean(x * x, axis=-1, keepdims=True)
    return x * lax.rsqrt(ms + EPS) * g


def _dot(a, b):
    return jnp.dot(a, b, preferred_element_type=F32)


def _resident(shape):
    zeros = (0,) * len(shape)
    return pl.BlockSpec(shape, lambda *_: zeros, pipeline_mode=pl.Buffered(1))


def _inproj_kernel(n_mix, n_gelu, x_ref, g_ref, w_ref, b_ref,
                   mix_ref, gelu_ref, gates_ref, h_ref):
    j = pl.program_id(1)

    @pl.when(j == 0)
    def _():
        h_ref[...] = _rms_norm(x_ref[...], g_ref[...]).astype(BF16)

    def project(epilogue, out_ref):
        for r in range(x_ref.shape[0] // ROW_SUB):
            rows = slice(r * ROW_SUB, (r + 1) * ROW_SUB)
            p = _dot(h_ref[rows, :], w_ref[...])
            out_ref[rows, :] = epilogue(p).astype(out_ref.dtype)

    @pl.when(j < n_mix)
    def _():
        project(lambda p: p, mix_ref)

    @pl.when(jnp.logical_and(j >= n_mix, j < n_mix + n_gelu))
    def _():
        project(jax.nn.gelu, gelu_ref)

    @pl.when(j >= n_mix + n_gelu)
    def _():
        project(lambda p: jax.nn.sigmoid(p + b_ref[...]), gates_ref)


def _inproj(x, g, w_in, b_gate, mix_width, gelu_width):
    n, d = x.shape
    in_width = w_in.shape[1]
    gate_width = in_width - mix_width - gelu_width
    tm, tn = INPROJ_TM, INPROJ_TN
    n_mix, n_gelu, n_gate = mix_width // tn, gelu_width // tn, gate_width // tn
    assert n % tm == 0 and n_mix * tn == mix_width and n_gelu * tn == gelu_width
    assert n_gate * tn == gate_width

    def clamp(j, lo, count):
        return jnp.clip(j - lo, 0, count - 1)

    return pl.pallas_call(
        functools.partial(_inproj_kernel, n_mix, n_gelu),
        grid=(n // tm, in_width // tn),
        in_specs=[
            pl.BlockSpec((tm, d), lambda i, j: (i, 0)),
            pl.BlockSpec((1, d), lambda i, j: (0, 0)),
            pl.BlockSpec((d, tn), lambda i, j: (0, j)),
            pl.BlockSpec((1, tn), lambda i, j: (0, clamp(j, n_mix + n_gelu, n_gate))),
        ],
        out_specs=[
            pl.BlockSpec((tm, tn), lambda i, j: (i, clamp(j, 0, n_mix))),
            pl.BlockSpec((tm, tn), lambda i, j: (i, clamp(j, n_mix, n_gelu))),
            pl.BlockSpec((tm, tn), lambda i, j: (i, clamp(j, n_mix + n_gelu, n_gate))),
        ],
        out_shape=[
            jax.ShapeDtypeStruct((n, mix_width), F32),
            jax.ShapeDtypeStruct((n, gelu_width), BF16),
            jax.ShapeDtypeStruct((n, gate_width), BF16),
        ],
        scratch_shapes=[pltpu.VMEM((tm, d), BF16)],
        compiler_params=pltpu.CompilerParams(
            dimension_semantics=("arbitrary", "arbitrary"),
            vmem_limit_bytes=VMEM_LIMIT_BYTES),
    )(x, g, w_in, b_gate)


def _mixer_kernel(pool_width, mix_ref, gelu_ref, wpool_ref, pscale_ref, convw_ref, convb_ref,
                  wa_ref, ba_ref, wi_ref, bi_ref, lam_ref,
                  ypool_ref, ylru_ref,
                  pext_ref, sub_ref, lext_ref, a_ref, bx_ref, h_ref, hcarry_ref):
    i = pl.program_id(1)
    t_rows = ypool_ref.shape[0]
    lru_width = ylru_ref.shape[1]

    @pl.when(i == 0)
    def _():
        pext_ref[0:POOL_HALO, :] = jnp.zeros((POOL_HALO, pool_width), F32)
        lext_ref[0:CONV_HALO, :] = jnp.zeros((CONV_HALO, lru_width), F32)
        hcarry_ref[...] = jnp.zeros_like(hcarry_ref)

    pext_ref[POOL_HALO:POOL_HALO + t_rows, :] = mix_ref[:, 0:pool_width]
    lext_ref[CONV_HALO:CONV_HALO + t_rows, :] = mix_ref[:, pool_width:]

    pos = (i * t_rows + 1 + lax.broadcasted_iota(jnp.int32, (t_rows, 1), 0)).astype(F32)
    for g, w in enumerate(POOL_WINDOWS):
        cols = slice(g * POOL_GROUP_WIDTH, (g + 1) * POOL_GROUP_WIDTH)

        def window(ref, q0, n, count, stride):
            s = ref[q0:q0 + n, cols]
            for m in range(1, count):
                s = s + ref[q0 - m * stride:q0 - m * stride + n, cols]
            return s

        if w <= POOL_SUB:
            s = window(pext_ref, POOL_HALO, t_rows, w, 1)
        else:
            q0 = POOL_HALO - POOL_SUB_HALO
            n = POOL_SUB_HALO + t_rows
            sub_ref[0:n, cols] = window(pext_ref, q0, n, POOL_SUB, 1)
            s = window(sub_ref, POOL_SUB_HALO, t_rows, w // POOL_SUB, POOL_SUB)
        inv_count = 1.0 / jnp.minimum(pos, float(w))
        d = s * inv_count - pext_ref[POOL_HALO:POOL_HALO + t_rows, cols]
        y = _dot(d.astype(BF16), wpool_ref[g]) * pscale_ref[:, cols]
        ypool_ref[:, cols] = y.astype(BF16)

    softplus_neg_lam = jax.nn.softplus(-lam_ref[...])
    n_taps = convw_ref.shape[0]
    for hd in range(lru_width // LRU_BLOCK_WIDTH):
        cols = slice(hd * LRU_BLOCK_WIDTH, (hd + 1) * LRU_BLOCK_WIDTH)
        cw = convw_ref[:, cols]
        v = convb_ref[:, cols]
        for k in range(n_taps):
            v = v + (lext_ref[CONV_HALO - k:CONV_HALO - k + t_rows, cols]
                     * cw[n_taps - 1 - k:n_taps - k, :])
        vb = v.astype(BF16)
        r = jax.nn.sigmoid(_dot(vb, wa_ref[hd]) + ba_ref[:, cols])
        ig = jax.nn.sigmoid(_dot(vb, wi_ref[hd]) + bi_ref[:, cols])
        log_a = (-LRU_C) * r * softplus_neg_lam[:, cols]
        a = jnp.exp(log_a)
        mult = jnp.sqrt(-jnp.tanh(log_a) * (a * a + 1.0))
        a_ref[:, cols] = a
        bx_ref[:, cols] = mult * (ig * v)

    def step(t, h):
        h = a_ref[pl.ds(t, 1), :] * h + bx_ref[pl.ds(t, 1), :]
        h_ref[pl.ds(t, 1), :] = h
        return h

    hcarry_ref[...] = lax.fori_loop(0, t_rows, step, hcarry_ref[...], unroll=8)
    ylru_ref[...] = (h_ref[...] * gelu_ref[...].astype(F32)).astype(BF16)

    pext_ref[0:POOL_HALO, :] = pext_ref[t_rows:t_rows + POOL_HALO, :]
    lext_ref[0:CONV_HALO, :] = lext_ref[t_rows:t_rows + CONV_HALO, :]


def _mixer(mix, act_gelu, w_pool, pool_scale, conv_w, conv_b, w_a, b_a, w_i, b_i, lam,
           batch, seq):
    n = mix.shape[0]
    pool_width = pool_scale.shape[1]
    lru_width = lam.shape[1]
    t = MIXER_T
    assert seq % t == 0
    nblk = seq // t
    row = lambda b, i: (b * nblk + i, 0)
    return pl.pallas_call(
        functools.partial(_mixer_kernel, pool_width),
        grid=(batch, nblk),
        in_specs=[
            pl.BlockSpec((t, pool_width + lru_width), row),
            pl.BlockSpec((t, lru_width), row),
            _resident(w_pool.shape), _resident(pool_scale.shape),
            _resident(conv_w.shape), _resident(conv_b.shape),
            _resident(w_a.shape), _resident(b_a.shape),
            _resident(w_i.shape), _resident(b_i.shape), _resident(lam.shape),
        ],
        out_specs=[pl.BlockSpec((t, pool_width), row), pl.BlockSpec((t, lru_width), row)],
        out_shape=[jax.ShapeDtypeStruct((n, pool_width), BF16),
                   jax.ShapeDtypeStruct((n, lru_width), BF16)],
        scratch_shapes=[
            pltpu.VMEM((POOL_HALO + t, pool_width), F32),
            pltpu.VMEM((POOL_SUB_HALO + t, pool_width), F32),
            pltpu.VMEM((CONV_HALO + t, lru_width), F32),
            pltpu.VMEM((t, lru_width), F32),
            pltpu.VMEM((t, lru_width), F32),
            pltpu.VMEM((t, lru_width), F32),
            pltpu.VMEM((1, lru_width), F32),
        ],
        compiler_params=pltpu.CompilerParams(
            dimension_semantics=("arbitrary", "arbitrary"),
            vmem_limit_bytes=VMEM_LIMIT_BYTES),
    )(mix, act_gelu, w_pool, pool_scale, conv_w, conv_b, w_a, b_a, w_i, b_i, lam)


def _merge_kernel(yp_ref, yl_ref, gates_ref, x_ref, wpp_ref, wlp_ref, wout_ref, o_ref, m_ref):
    d = o_ref.shape[1]
    ch = MERGE_CHUNK
    for c in range(d // ch):
        cols = slice(c * ch, (c + 1) * ch)
        p = _dot(yp_ref[...], wpp_ref[:, cols])
        q = _dot(yl_ref[...], wlp_ref[:, cols])
        g_pool = gates_ref[:, cols].astype(F32)
        g_lru = gates_ref[:, d + c * ch:d + (c + 1) * ch].astype(F32)
        m_ref[:, cols] = (g_pool * p + g_lru * q).astype(BF16)
    for c in range(d // ch):
        cols = slice(c * ch, (c + 1) * ch)
        o_ref[:, cols] = x_ref[:, cols] + _dot(m_ref[...], wout_ref[:, cols])


def _merge(y_pool, y_lru, gates, x, w_pool_proj, w_lru_proj, w_out):
    n, d = x.shape
    tm = MERGE_TM
    assert n % tm == 0 and d % MERGE_CHUNK == 0
    row = lambda i: (i, 0)
    return pl.pallas_call(
        _merge_kernel,
        grid=(n // tm,),
        in_specs=[
            pl.BlockSpec((tm, y_pool.shape[1]), row),
            pl.BlockSpec((tm, y_lru.shape[1]), row),
            pl.BlockSpec((tm, gates.shape[1]), row),
            pl.BlockSpec((tm, d), row),
            _resident(w_pool_proj.shape), _resident(w_lru_proj.shape), _resident(w_out.shape),
        ],
        out_specs=pl.BlockSpec((tm, d), row),
        out_shape=jax.ShapeDtypeStruct((n, d), F32),
        scratch_shapes=[pltpu.VMEM((tm, d), BF16)],
        compiler_params=pltpu.CompilerParams(
            dimension_semantics=("arbitrary",),
            vmem_limit_bytes=VMEM_LIMIT_BYTES),
    )(y_pool, y_lru, gates, x, w_pool_proj, w_lru_proj, w_out)


def _ffn_kernel(apply_final, x_ref, g_ref, wg_ref, wv_ref, cw_ref, cb_ref, wd_ref, gfin_ref,
                o_ref, h2_ref, acc_ref, ext_ref, carry_ref):
    i = pl.program_id(1)
    f = pl.program_id(2)
    t_rows = x_ref.shape[0]

    @pl.when(f == 0)
    def _():
        h2_ref[...] = _rms_norm(x_ref[...], g_ref[...]).astype(BF16)
        acc_ref[...] = jnp.zeros_like(acc_ref)

    @pl.when(i == 0)
    def _():
        ext_ref[0:CONV_HALO, :] = jnp.zeros((CONV_HALO, ext_ref.shape[1]), F32)

    @pl.when(i > 0)
    def _():
        ext_ref[0:CONV_HALO, :] = carry_ref[f]

    n_taps = cw_ref.shape[0]
    for r in range(t_rows // ROW_SUB):
        rows = slice(r * ROW_SUB, (r + 1) * ROW_SUB)
        base = CONV_HALO + r * ROW_SUB
        ext_ref[base:base + ROW_SUB, :] = _dot(h2_ref[rows, :], wg_ref[...])
        val = _dot(h2_ref[rows, :], wv_ref[...])
        conv = cb_ref[...]
        for k in range(n_taps):
            conv = conv + (ext_ref[base - k:base - k + ROW_SUB, :]
                           * cw_ref[n_taps - 1 - k:n_taps - k, :])
        act = (jax.nn.gelu(conv) * val).astype(BF16)
        acc_ref[rows, :] += _dot(act, wd_ref[...])

    carry_ref[f] = ext_ref[t_rows:t_rows + CONV_HALO, :]

    @pl.when(f == pl.num_programs(2) - 1)
    def _():
        y = x_ref[...] + acc_ref[...]
        if apply_final:
            y = _rms_norm(y, gfin_ref[...])
        o_ref[...] = y


def _ffn(x, g, w_up, conv_w, conv_b, w_down, g_final, apply_final, batch, seq):
    n, d = x.shape
    d_ff = w_down.shape[0]
    t, tf = FFN_T, FFN_TF
    assert seq % t == 0 and d_ff % tf == 0
    nblk, nf = seq // t, d_ff // tf
    row = lambda b, i, f: (b * nblk + i, 0)
    return pl.pallas_call(
        functools.partial(_ffn_kernel, apply_final),
        grid=(batch, nblk, nf),
        in_specs=[
            pl.BlockSpec((t, d), row),
            pl.BlockSpec((1, d), lambda b, i, f: (0, 0)),
            pl.BlockSpec((d, tf), lambda b, i, f: (0, f)),
            pl.BlockSpec((d, tf), lambda b, i, f: (0, nf + f)),
            pl.BlockSpec((conv_w.shape[0], tf), lambda b, i, f: (0, f)),
            pl.BlockSpec((1, tf), lambda b, i, f: (0, f)),
            pl.BlockSpec((tf, d), lambda b, i, f: (f, 0)),
            pl.BlockSpec((1, d), lambda b, i, f: (0, 0)),
        ],
        out_specs=pl.BlockSpec((t, d), row),
        out_shape=jax.ShapeDtypeStruct((n, d), F32),
        scratch_shapes=[
            pltpu.VMEM((t, d), BF16),
            pltpu.VMEM((t, d), F32),
            pltpu.VMEM((CONV_HALO + t, tf), F32),
            pltpu.VMEM((nf, CONV_HALO, tf), F32),
        ],
        compiler_params=pltpu.CompilerParams(
            dimension_semantics=("arbitrary", "arbitrary", "arbitrary"),
            vmem_limit_bytes=VMEM_LIMIT_BYTES),
    )(x, g, w_up, w_up, conv_w, conv_b, w_down, g_final)


def kernel(x, g_mix, w_in, b_gate, w_pool, pool_scale, lru_conv_w, lru_conv_b, w_a, b_a, w_i,
           b_i, lru_lambda, w_pool_proj, w_lru_proj, w_out, g_mlp, w_up, ffn_conv_w, ffn_conv_b,
           w_down, g_final):
    batch, seq, d = x.shape
    depth = w_in.shape[0]
    pool_width = pool_scale.shape[1]
    lru_width = lru_lambda.shape[1]
    row2 = lambda v: v.reshape(1, -1)
    xf = x.reshape(batch * seq, d)
    for l in range(depth):
        mix, act_gelu, gates = _inproj(
            xf, row2(g_mix[l]), w_in[l].astype(BF16), row2(b_gate[l]),
            pool_width + lru_width, lru_width)
        y_pool, y_lru = _mixer(
            mix, act_gelu, w_pool[l].astype(BF16), row2(pool_scale[l]),
            lru_conv_w[l], row2(lru_conv_b[l]), w_a[l].astype(BF16), row2(b_a[l]),
            w_i[l].astype(BF16), row2(b_i[l]), row2(lru_lambda[l]), batch, seq)
        xf = _merge(y_pool, y_lru, gates, xf, w_pool_proj[l].astype(BF16),
                    w_lru_proj[l].astype(BF16), w_out[l].astype(BF16))
        xf = _ffn(xf, row2(g_mlp[l]), w_up[l].astype(BF16), ffn_conv_w[l], row2(ffn_conv_b[l]),
                  w_down[l].astype(BF16), row2(g_final), l == depth - 1, batch, seq)
    return xf.reshape(batch, seq, d)
```

```python
import functools

import jax
import jax.numpy as jnp
from jax import lax
from jax.experimental import pallas as pl
from jax.experimental.pallas import tpu as pltpu

F32 = jnp.float32
BF16 = jnp.bfloat16

POOL_WINDOWS = (2, 4, 8, 16)
POOL_GROUP_WIDTH = 256
LRU_BLOCK_WIDTH = 256
LRU_C = 8.0
EPS = 1e-6

SUBLANES = 8
BF16_SUBLANES = 16
POOL_SUB = 4
POOL_SUB_HALO = 16
POOL_HALO = 24
CONV_HALO = SUBLANES
VMEM_LIMIT_BYTES = 56 * 1024 * 1024

ROW_SUB = 256
INPROJ_TM = 1024
INPROJ_TN = 1024
MIXER_T = 512
MERGE_TM = 256
MERGE_CHUNK = 512
FFN_T = 512
FFN_TF = 1024


def _rms_norm(x, g):
    ms = jnp.mean(x * x, axis=-1, keepdims=True)
    return x * lax.rsqrt(ms + EPS) * g


def _dot(a, b):
    return jnp.dot(a, b, preferred_element_type=F32)


def _resident(shape):
    zeros = (0,) * len(shape)
    return pl.BlockSpec(shape, lambda *_: zeros, pipeline_mode=pl.Buffered(1))


def _slab_specs(weights, n_steps, step_index):
    specs = []
    for w in weights:
        rows, cols = w.shape
        assert rows % (n_steps * BF16_SUBLANES) == 0
        specs.append(pl.BlockSpec((rows // n_steps, cols), step_index))
    return specs


def _cast_slabs(src_refs, dst_refs):
    for src, dst in zip(src_refs, dst_refs):
        dst[...] = src[...].astype(BF16)


def _inproj_kernel(n_mix, n_gelu, x_ref, g_ref, w_ref, b_ref,
                   mix_ref, gelu_ref, gates_ref, h_ref):
    j = pl.program_id(1)

    @pl.when(j == 0)
    def _():
        h_ref[...] = _rms_norm(x_ref[...], g_ref[...]).astype(BF16)

    def project(epilogue, out_ref):
        for r in range(x_ref.shape[0] // ROW_SUB):
            rows = slice(r * ROW_SUB, (r + 1) * ROW_SUB)
            p = _dot(h_ref[rows, :], w_ref[...])
            out_ref[rows, :] = epilogue(p).astype(out_ref.dtype)

    @pl.when(j < n_mix)
    def _():
        project(lambda p: p, mix_ref)

    @pl.when(jnp.logical_and(j >= n_mix, j < n_mix + n_gelu))
    def _():
        project(jax.nn.gelu, gelu_ref)

    @pl.when(j >= n_mix + n_gelu)
    def _():
        project(lambda p: jax.nn.sigmoid(p + b_ref[...]), gates_ref)


def _inproj(x, g, w_in, b_gate, mix_width, gelu_width):
    n, d = x.shape
    in_width = w_in.shape[1]
    gate_width = in_width - mix_width - gelu_width
    tm, tn = INPROJ_TM, INPROJ_TN
    n_mix, n_gelu, n_gate = mix_width // tn, gelu_width // tn, gate_width // tn
    assert n % tm == 0 and n_mix * tn == mix_width and n_gelu * tn == gelu_width
    assert n_gate * tn == gate_width

    def clamp(j, lo, count):
        return jnp.clip(j - lo, 0, count - 1)

    return pl.pallas_call(
        functools.partial(_inproj_kernel, n_mix, n_gelu),
        grid=(n // tm, in_width // tn),
        in_specs=[
            pl.BlockSpec((tm, d), lambda i, j: (i, 0)),
            pl.BlockSpec((1, d), lambda i, j: (0, 0)),
            pl.BlockSpec((d, tn), lambda i, j: (0, j)),
            pl.BlockSpec((1, tn), lambda i, j: (0, clamp(j, n_mix + n_gelu, n_gate))),
        ],
        out_specs=[
            pl.BlockSpec((tm, tn), lambda i, j: (i, clamp(j, 0, n_mix))),
            pl.BlockSpec((tm, tn), lambda i, j: (i, clamp(j, n_mix, n_gelu))),
            pl.BlockSpec((tm, tn), lambda i, j: (i, clamp(j, n_mix + n_gelu, n_gate))),
        ],
        out_shape=[
            jax.ShapeDtypeStruct((n, mix_width), F32),
            jax.ShapeDtypeStruct((n, gelu_width), BF16),
            jax.ShapeDtypeStruct((n, gate_width), BF16),
        ],
        scratch_shapes=[pltpu.VMEM((tm, d), BF16)],
        compiler_params=pltpu.CompilerParams(
            dimension_semantics=("arbitrary", "arbitrary"),
            vmem_limit_bytes=VMEM_LIMIT_BYTES),
    )(x, g, w_in, b_gate)


def _mixer_kernel(pool_width, mix_ref, gelu_ref, wpool_ref, pscale_ref, convw_ref, convb_ref,
                  wa_ref, ba_ref, wi_ref, bi_ref, lam_ref, wpp_ref, wlp_ref, wout_ref,
                  ypool_ref, ylru_ref, wpp_bf_ref, wlp_bf_ref, wout_bf_ref,
                  pext_ref, sub_ref, lext_ref, a_ref, bx_ref, hcarry_ref):
    _cast_slabs((wpp_ref, wlp_ref, wout_ref), (wpp_bf_ref, wlp_bf_ref, wout_bf_ref))

    i = pl.program_id(1)
    t_rows = ypool_ref.shape[0]
    lru_width = ylru_ref.shape[1]

    @pl.when(i == 0)
    def _():
        pext_ref[0:POOL_HALO, :] = jnp.zeros((POOL_HALO, pool_width), F32)
        lext_ref[0:CONV_HALO, :] = jnp.zeros((CONV_HALO, lru_width), F32)
        hcarry_ref[...] = jnp.zeros_like(hcarry_ref)

    pext_ref[POOL_HALO:POOL_HALO + t_rows, :] = mix_ref[:, 0:pool_width]
    lext_ref[CONV_HALO:CONV_HALO + t_rows, :] = mix_ref[:, pool_width:]

    pos = (i * t_rows + 1 + lax.broadcasted_iota(jnp.int32, (t_rows, 1), 0)).astype(F32)
    for g, w in enumerate(POOL_WINDOWS):
        cols = slice(g * POOL_GROUP_WIDTH, (g + 1) * POOL_GROUP_WIDTH)

        def window(ref, q0, n, count, stride):
            s = ref[q0:q0 + n, cols]
            for m in range(1, count):
                s = s + ref[q0 - m * stride:q0 - m * stride + n, cols]
            return s

        if w <= POOL_SUB:
            s = window(pext_ref, POOL_HALO, t_rows, w, 1)
        else:
            q0 = POOL_HALO - POOL_SUB_HALO
            n = POOL_SUB_HALO + t_rows
            sub_ref[0:n, cols] = window(pext_ref, q0, n, POOL_SUB, 1)
            s = window(sub_ref, POOL_SUB_HALO, t_rows, w // POOL_SUB, POOL_SUB)
        inv_count = 1.0 / jnp.minimum(pos, float(w))
        d = s * inv_count - pext_ref[POOL_HALO:POOL_HALO + t_rows, cols]
        y = _dot(d.astype(BF16), wpool_ref[g]) * pscale_ref[:, cols]
        ypool_ref[:, cols] = y.astype(BF16)

    softplus_neg_lam = jax.nn.softplus(-lam_ref[...])
    n_taps = convw_ref.shape[0]
    for hd in range(lru_width // LRU_BLOCK_WIDTH):
        cols = slice(hd * LRU_BLOCK_WIDTH, (hd + 1) * LRU_BLOCK_WIDTH)
        cw = convw_ref[:, cols]
        v = convb_ref[:, cols]
        for k in range(n_taps):
            v = v + (lext_ref[CONV_HALO - k:CONV_HALO - k + t_rows, cols]
                     * cw[n_taps - 1 - k:n_taps - k, :])
        vb = v.astype(BF16)
        r = jax.nn.sigmoid(_dot(vb, wa_ref[hd]) + ba_ref[:, cols])
        ig = jax.nn.sigmoid(_dot(vb, wi_ref[hd]) + bi_ref[:, cols])
        log_a = (-LRU_C) * r * softplus_neg_lam[:, cols]
        a = jnp.exp(log_a)
        mult = jnp.sqrt(-jnp.tanh(log_a) * (a * a + 1.0))
        a_ref[:, cols] = a
        bx_ref[:, cols] = mult * (ig * v)

    def step(t, h):
        h = a_ref[pl.ds(t, 1), :] * h + bx_ref[pl.ds(t, 1), :]
        bx_ref[pl.ds(t, 1), :] = h
        return h

    hcarry_ref[...] = lax.fori_loop(0, t_rows, step, hcarry_ref[...], unroll=8)
    ylru_ref[...] = (bx_ref[...] * gelu_ref[...].astype(F32)).astype(BF16)

    pext_ref[0:POOL_HALO, :] = pext_ref[t_rows:t_rows + POOL_HALO, :]
    lext_ref[0:CONV_HALO, :] = lext_ref[t_rows:t_rows + CONV_HALO, :]


def _mixer(mix, act_gelu, w_pool, pool_scale, conv_w, conv_b, w_a, b_a, w_i, b_i, lam,
           cast_weights, batch, seq):
    n = mix.shape[0]
    pool_width = pool_scale.shape[1]
    lru_width = lam.shape[1]
    t = MIXER_T
    assert seq % t == 0
    nblk = seq // t
    row = lambda b, i: (b * nblk + i, 0)
    return pl.pallas_call(
        functools.partial(_mixer_kernel, pool_width),
        grid=(batch, nblk),
        in_specs=[
            pl.BlockSpec((t, pool_width + lru_width), row),
            pl.BlockSpec((t, lru_width), row),
            _resident(w_pool.shape), _resident(pool_scale.shape),
            _resident(conv_w.shape), _resident(conv_b.shape),
            _resident(w_a.shape), _resident(b_a.shape),
            _resident(w_i.shape), _resident(b_i.shape), _resident(lam.shape),
            *_slab_specs(cast_weights, batch * nblk, row),
        ],
        out_specs=[pl.BlockSpec((t, pool_width), row), pl.BlockSpec((t, lru_width), row),
                   *_slab_specs(cast_weights, batch * nblk, row)],
        out_shape=[jax.ShapeDtypeStruct((n, pool_width), BF16),
                   jax.ShapeDtypeStruct((n, lru_width), BF16),
                   *[jax.ShapeDtypeStruct(w.shape, BF16) for w in cast_weights]],
        scratch_shapes=[
            pltpu.VMEM((POOL_HALO + t, pool_width), F32),
            pltpu.VMEM((POOL_SUB_HALO + t, pool_width), F32),
            pltpu.VMEM((CONV_HALO + t, lru_width), F32),
            pltpu.VMEM((t, lru_width), F32),
            pltpu.VMEM((t, lru_width), F32),
            pltpu.VMEM((1, lru_width), F32),
        ],
        compiler_params=pltpu.CompilerParams(
            dimension_semantics=("arbitrary", "arbitrary"),
            vmem_limit_bytes=VMEM_LIMIT_BYTES),
    )(mix, act_gelu, w_pool, pool_scale, conv_w, conv_b, w_a, b_a, w_i, b_i, lam, *cast_weights)


def _merge_kernel(yp_ref, yl_ref, gates_ref, x_ref, wpp_ref, wlp_ref, wout_ref, wup_ref, wdn_ref,
                  o_ref, wup_bf_ref, wdn_bf_ref, m_ref):
    _cast_slabs((wup_ref, wdn_ref), (wup_bf_ref, wdn_bf_ref))
    d = o_ref.shape[1]
    ch = MERGE_CHUNK
    for c in range(d // ch):
        cols = slice(c * ch, (c + 1) * ch)
        p = _dot(yp_ref[...], wpp_ref[:, cols])
        q = _dot(yl_ref[...], wlp_ref[:, cols])
        g_pool = gates_ref[:, cols].astype(F32)
        g_lru = gates_ref[:, d + c * ch:d + (c + 1) * ch].astype(F32)
        m_ref[:, cols] = (g_pool * p + g_lru * q).astype(BF16)
    for c in range(d // ch):
        cols = slice(c * ch, (c + 1) * ch)
        o_ref[:, cols] = x_ref[:, cols] + _dot(m_ref[...], wout_ref[:, cols])


def _merge(y_pool, y_lru, gates, x, w_pool_proj, w_lru_proj, w_out, cast_weights):
    n, d = x.shape
    tm = MERGE_TM
    assert n % tm == 0 and d % MERGE_CHUNK == 0
    row = lambda i: (i, 0)
    return pl.pallas_call(
        _merge_kernel,
        grid=(n // tm,),
        in_specs=[
            pl.BlockSpec((tm, y_pool.shape[1]), row),
            pl.BlockSpec((tm, y_lru.shape[1]), row),
            pl.BlockSpec((tm, gates.shape[1]), row),
            pl.BlockSpec((tm, d), row),
            _resident(w_pool_proj.shape), _resident(w_lru_proj.shape), _resident(w_out.shape),
            *_slab_specs(cast_weights, n // tm, row),
        ],
        out_specs=[pl.BlockSpec((tm, d), row), *_slab_specs(cast_weights, n // tm, row)],
        out_shape=[jax.ShapeDtypeStruct((n, d), F32),
                   *[jax.ShapeDtypeStruct(w.shape, BF16) for w in cast_weights]],
        scratch_shapes=[pltpu.VMEM((tm, d), BF16)],
        compiler_params=pltpu.CompilerParams(
            dimension_semantics=("arbitrary",),
            vmem_limit_bytes=VMEM_LIMIT_BYTES),
    )(y_pool, y_lru, gates, x, w_pool_proj, w_lru_proj, w_out, *cast_weights)


def _ffn_kernel(apply_final, x_ref, g_ref, wg_ref, wv_ref, cw_ref, cb_ref, wd_ref, gfin_ref,
                o_ref, h2_ref, acc_ref, ext_ref, carry_ref):
    i = pl.program_id(1)
    f = pl.program_id(2)
    t_rows = x_ref.shape[0]

    @pl.when(f == 0)
    def _():
        h2_ref[...] = _rms_norm(x_ref[...], g_ref[...]).astype(BF16)
        acc_ref[...] = jnp.zeros_like(acc_ref)

    @pl.when(i == 0)
    def _():
        ext_ref[0:CONV_HALO, :] = jnp.zeros((CONV_HALO, ext_ref.shape[1]), F32)

    @pl.when(i > 0)
    def _():
        ext_ref[0:CONV_HALO, :] = carry_ref[f]

    n_taps = cw_ref.shape[0]
    for r in range(t_rows // ROW_SUB):
        rows = slice(r * ROW_SUB, (r + 1) * ROW_SUB)
        base = CONV_HALO + r * ROW_SUB
        ext_ref[base:base + ROW_SUB, :] = _dot(h2_ref[rows, :], wg_ref[...])
        val = _dot(h2_ref[rows, :], wv_ref[...])
        conv = cb_ref[...]
        for k in range(n_taps):
            conv = conv + (ext_ref[base - k:base - k + ROW_SUB, :]
                           * cw_ref[n_taps - 1 - k:n_taps - k, :])
        act = (jax.nn.gelu(conv) * val).astype(BF16)
        acc_ref[rows, :] += _dot(act, wd_ref[...])

    carry_ref[f] = ext_ref[t_rows:t_rows + CONV_HALO, :]

    @pl.when(f == pl.num_programs(2) - 1)
    def _():
        y = x_ref[...] + acc_ref[...]
        if apply_final:
            y = _rms_norm(y, gfin_ref[...])
        o_ref[...] = y


def _ffn(x, g, w_up, conv_w, conv_b, w_down, g_final, apply_final, batch, seq):
    n, d = x.shape
    d_ff = w_down.shape[0]
    t, tf = FFN_T, FFN_TF
    assert seq % t == 0 and d_ff % tf == 0
    nblk, nf = seq // t, d_ff // tf
    row = lambda b, i, f: (b * nblk + i, 0)
    return pl.pallas_call(
        functools.partial(_ffn_kernel, apply_final),
        grid=(batch, nblk, nf),
        in_specs=[
            pl.BlockSpec((t, d), row),
            pl.BlockSpec((1, d), lambda b, i, f: (0, 0)),
            pl.BlockSpec((d, tf), lambda b, i, f: (0, f)),
            pl.BlockSpec((d, tf), lambda b, i, f: (0, nf + f)),
            pl.BlockSpec((conv_w.shape[0], tf), lambda b, i, f: (0, f)),
            pl.BlockSpec((1, tf), lambda b, i, f: (0, f)),
            pl.BlockSpec((tf, d), lambda b, i, f: (f, 0)),
            pl.BlockSpec((1, d), lambda b, i, f: (0, 0)),
        ],
        out_specs=pl.BlockSpec((t, d), row),
        out_shape=jax.ShapeDtypeStruct((n, d), F32),
        scratch_shapes=[
            pltpu.VMEM((t, d), BF16),
            pltpu.VMEM((t, d), F32),
            pltpu.VMEM((CONV_HALO + t, tf), F32),
            pltpu.VMEM((nf, CONV_HALO, tf), F32),
        ],
        compiler_params=pltpu.CompilerParams(
            dimension_semantics=("arbitrary", "arbitrary", "arbitrary"),
            vmem_limit_bytes=VMEM_LIMIT_BYTES),
    )(x, g, w_up, w_up, conv_w, conv_b, w_down, g_final)


def kernel(x, g_mix, w_in, b_gate, w_pool, pool_scale, lru_conv_w, lru_conv_b, w_a, b_a, w_i,
           b_i, lru_lambda, w_pool_proj, w_lru_proj, w_out, g_mlp, w_up, ffn_conv_w, ffn_conv_b,
           w_down, g_final):
    batch, seq, d = x.shape
    depth = w_in.shape[0]
    pool_width = pool_scale.shape[1]
    lru_width = lru_lambda.shape[1]
    row2 = lambda v: v.reshape(1, -1)
    xf = x.reshape(batch * seq, d)
    for l in range(depth):
        mix, act_gelu, gates = _inproj(
            xf, row2(g_mix[l]), w_in[l].astype(BF16), row2(b_gate[l]),
            pool_width + lru_width, lru_width)
        y_pool, y_lru, wpp_bf, wlp_bf, wout_bf = _mixer(
            mix, act_gelu, w_pool[l].astype(BF16), row2(pool_scale[l]),
            lru_conv_w[l], row2(lru_conv_b[l]), w_a[l].astype(BF16), row2(b_a[l]),
            w_i[l].astype(BF16), row2(b_i[l]), row2(lru_lambda[l]),
            (w_pool_proj[l], w_lru_proj[l], w_out[l]), batch, seq)
        xf, wup_bf, wdn_bf = _merge(y_pool, y_lru, gates, xf, wpp_bf, wlp_bf, wout_bf,
                                    (w_up[l], w_down[l]))
        xf = _ffn(xf, row2(g_mlp[l]), wup_bf, ffn_conv_w[l], row2(ffn_conv_b[l]),
                  wdn_bf, row2(g_final), l == depth - 1, batch, seq)
    return xf.reshape(batch, seq, d)
```

```python
import functools

import jax
import jax.numpy as jnp
from jax import lax
from jax.experimental import pallas as pl
from jax.experimental.pallas import tpu as pltpu

F32 = jnp.float32
BF16 = jnp.bfloat16

POOL_WINDOWS = (2, 4, 8, 16)
POOL_GROUP_WIDTH = 256
LRU_BLOCK_WIDTH = 256
LRU_C = 8.0
EPS = 1e-6

SUBLANES = 8
BF16_SUBLANES = 16
POOL_SUB = 4
POOL_SUB_HALO = 16
SEQ_HALO = 24
CONV_HALO = SUBLANES
VMEM_LIMIT_BYTES = 56 * 1024 * 1024

ROW_SUB = 256
INPROJ_TM = 1024
INPROJ_TN = 1024
MIXER_T = 512
MERGE_TM = 256
MERGE_CHUNK = 512
FFN_T = 512
FFN_TF = 1024


def _rms_norm(x, g):
    ms = jnp.mean(x * x, axis=-1, keepdims=True)
    return x * lax.rsqrt(ms + EPS) * g


def _dot(a, b):
    return jnp.dot(a, b, preferred_element_type=F32)


def _resident(shape):
    zeros = (0,) * len(shape)
    return pl.BlockSpec(shape, lambda *_: zeros, pipeline_mode=pl.Buffered(1))


def _slab_specs(weights, n_steps, step_index):
    specs = []
    for w in weights:
        rows, cols = w.shape
        assert rows % (n_steps * BF16_SUBLANES) == 0
        specs.append(pl.BlockSpec((rows // n_steps, cols), step_index))
    return specs


def _cast_slabs(src_refs, dst_refs):
    for src, dst in zip(src_refs, dst_refs):
        dst[...] = src[...].astype(BF16)


def _inproj_kernel(n_lru, n_gelu, blocks_per_seq,
                   x_ref, g_ref, w_ref, b_ref, convw_ref, convb_ref,
                   d_ref, v_ref, gelu_ref, gates_ref,
                   h_ref, ext_ref, sub_ref, carry_ref):
    i = pl.program_id(0)
    j = pl.program_id(1)
    n_sub = x_ref.shape[0] // ROW_SUB
    block_in_seq = lax.rem(i, blocks_per_seq)

    @pl.when(j == 0)
    def _():
        h_ref[...] = _rms_norm(x_ref[...], g_ref[...]).astype(BF16)

    def project(r):
        return _dot(h_ref[r * ROW_SUB:(r + 1) * ROW_SUB, :], w_ref[...])

    def load_halo():
        @pl.when(block_in_seq == 0)
        def _():
            ext_ref[0, 0:SEQ_HALO, :] = jnp.zeros((SEQ_HALO, ext_ref.shape[2]), F32)

        @pl.when(block_in_seq > 0)
        def _():
            ext_ref[0, 0:SEQ_HALO, :] = carry_ref[j]

    def pass_tail(r):
        tail = ext_ref[r % 2, ROW_SUB:ROW_SUB + SEQ_HALO, :]
        if r + 1 < n_sub:
            ext_ref[(r + 1) % 2, 0:SEQ_HALO, :] = tail
        else:
            carry_ref[j] = tail

    def window(load, q0, n, count, stride):
        s = load(q0, n)
        for m in range(1, count):
            s = s + load(q0 - m * stride, n)
        return s

    @pl.when(j == 0)
    def _():
        load_halo()
        wide = [g for g, w in enumerate(POOL_WINDOWS) if w > POOL_SUB]
        for r in range(n_sub):
            slot = r % 2
            rows = slice(r * ROW_SUB, (r + 1) * ROW_SUB)
            ext_ref[slot, SEQ_HALO:SEQ_HALO + ROW_SUB, :] = project(r)
            pos = (block_in_seq * x_ref.shape[0] + r * ROW_SUB + 1
                   + lax.broadcasted_iota(jnp.int32, (ROW_SUB, 1), 0)).astype(F32)
            for g, w in enumerate(POOL_WINDOWS):
                cols = slice(g * POOL_GROUP_WIDTH, (g + 1) * POOL_GROUP_WIDTH)
                load_ext = lambda q0, n: ext_ref[slot, q0:q0 + n, cols]
                if w <= POOL_SUB:
                    s = window(load_ext, SEQ_HALO, ROW_SUB, w, 1)
                else:
                    scols = slice(wide.index(g) * POOL_GROUP_WIDTH,
                                  (wide.index(g) + 1) * POOL_GROUP_WIDTH)
                    n = POOL_SUB_HALO + ROW_SUB
                    sub_ref[slot, 0:n, scols] = window(
                        load_ext, SEQ_HALO - POOL_SUB_HALO, n, POOL_SUB, 1)
                    load_sub = lambda q0, n: sub_ref[slot, q0:q0 + n, scols]
                    s = window(load_sub, POOL_SUB_HALO, ROW_SUB, w // POOL_SUB, POOL_SUB)
                inv_count = 1.0 / jnp.minimum(pos, float(w))
                d = s * inv_count - ext_ref[slot, SEQ_HALO:SEQ_HALO + ROW_SUB, cols]
                d_ref[rows, cols] = d.astype(BF16)
            pass_tail(r)

    @pl.when(jnp.logical_and(j >= 1, j < 1 + n_lru))
    def _():
        load_halo()
        n_taps = convw_ref.shape[0]
        for r in range(n_sub):
            slot = r % 2
            ext_ref[slot, SEQ_HALO:SEQ_HALO + ROW_SUB, :] = project(r)
            v = convb_ref[...]
            for k in range(n_taps):
                v = v + (ext_ref[slot, SEQ_HALO - k:SEQ_HALO - k + ROW_SUB, :]
                         * convw_ref[n_taps - 1 - k:n_taps - k, :])
            v_ref[r * ROW_SUB:(r + 1) * ROW_SUB, :] = v
            pass_tail(r)

    @pl.when(jnp.logical_and(j >= 1 + n_lru, j < 1 + n_lru + n_gelu))
    def _():
        for r in range(n_sub):
            gelu_ref[r * ROW_SUB:(r + 1) * ROW_SUB, :] = jax.nn.gelu(project(r)).astype(BF16)

    @pl.when(j >= 1 + n_lru + n_gelu)
    def _():
        for r in range(n_sub):
            gates_ref[r * ROW_SUB:(r + 1) * ROW_SUB, :] = (
                jax.nn.sigmoid(project(r) + b_ref[...]).astype(BF16))


def _inproj(x, g, w_in, b_gate, conv_w, conv_b, pool_width, lru_width, seq):
    n, d = x.shape
    in_width = w_in.shape[1]
    gate_width = in_width - pool_width - 2 * lru_width
    tm, tn = INPROJ_TM, INPROJ_TN
    n_lru, n_gate = lru_width // tn, gate_width // tn
    assert pool_width == tn == len(POOL_WINDOWS) * POOL_GROUP_WIDTH
    assert n % tm == 0 and seq % tm == 0 and tm % ROW_SUB == 0
    assert n_lru * tn == lru_width and n_gate * tn == gate_width
    n_wide = sum(w > POOL_SUB for w in POOL_WINDOWS)

    def clamp(j, lo, count):
        return jnp.clip(j - lo, 0, count - 1)

    lru_tile = lambda i, j: (0, clamp(j, 1, n_lru))
    return pl.pallas_call(
        functools.partial(_inproj_kernel, n_lru, n_lru, seq // tm),
        grid=(n // tm, in_width // tn),
        in_specs=[
            pl.BlockSpec((tm, d), lambda i, j: (i, 0)),
            pl.BlockSpec((1, d), lambda i, j: (0, 0)),
            pl.BlockSpec((d, tn), lambda i, j: (0, j)),
            pl.BlockSpec((1, tn), lambda i, j: (0, clamp(j, 1 + 2 * n_lru, n_gate))),
            pl.BlockSpec((conv_w.shape[0], tn), lru_tile),
            pl.BlockSpec((1, tn), lru_tile),
        ],
        out_specs=[
            pl.BlockSpec((tm, tn), lambda i, j: (i, 0)),
            pl.BlockSpec((tm, tn), lambda i, j: (i, clamp(j, 1, n_lru))),
            pl.BlockSpec((tm, tn), lambda i, j: (i, clamp(j, 1 + n_lru, n_lru))),
            pl.BlockSpec((tm, tn), lambda i, j: (i, clamp(j, 1 + 2 * n_lru, n_gate))),
        ],
        out_shape=[
            jax.ShapeDtypeStruct((n, pool_width), BF16),
            jax.ShapeDtypeStruct((n, lru_width), F32),
            jax.ShapeDtypeStruct((n, lru_width), BF16),
            jax.ShapeDtypeStruct((n, gate_width), BF16),
        ],
        scratch_shapes=[
            pltpu.VMEM((tm, d), BF16),
            pltpu.VMEM((2, SEQ_HALO + ROW_SUB, tn), F32),
            pltpu.VMEM((2, POOL_SUB_HALO + ROW_SUB, n_wide * POOL_GROUP_WIDTH), F32),
            pltpu.VMEM((1 + n_lru, SEQ_HALO, tn), F32),
        ],
        compiler_params=pltpu.CompilerParams(
            dimension_semantics=("arbitrary", "arbitrary"),
            vmem_limit_bytes=VMEM_LIMIT_BYTES),
    )(x, g, w_in, b_gate, conv_w, conv_b)


def _mixer_kernel(d_ref, v_ref, gelu_ref, wpool_ref, pscale_ref,
                  wa_ref, ba_ref, wi_ref, bi_ref, lam_ref, wpp_ref, wlp_ref, wout_ref,
                  ypool_ref, ylru_ref, wpp_bf_ref, wlp_bf_ref, wout_bf_ref,
                  a_ref, bx_ref, h_ref, hcarry_ref):
    _cast_slabs((wpp_ref, wlp_ref, wout_ref), (wpp_bf_ref, wlp_bf_ref, wout_bf_ref))

    t_rows = ylru_ref.shape[0]
    lru_width = ylru_ref.shape[1]

    @pl.when(pl.program_id(1) == 0)
    def _():
        hcarry_ref[...] = jnp.zeros_like(hcarry_ref)

    for g in range(len(POOL_WINDOWS)):
        cols = slice(g * POOL_GROUP_WIDTH, (g + 1) * POOL_GROUP_WIDTH)
        y = _dot(d_ref[:, cols], wpool_ref[g]) * pscale_ref[:, cols]
        ypool_ref[:, cols] = y.astype(BF16)

    softplus_neg_lam = jax.nn.softplus(-lam_ref[...])
    for hd in range(lru_width // LRU_BLOCK_WIDTH):
        cols = slice(hd * LRU_BLOCK_WIDTH, (hd + 1) * LRU_BLOCK_WIDTH)
        v = v_ref[:, cols]
        vb = v.astype(BF16)
        r = jax.nn.sigmoid(_dot(vb, wa_ref[hd]) + ba_ref[:, cols])
        ig = jax.nn.sigmoid(_dot(vb, wi_ref[hd]) + bi_ref[:, cols])
        log_a = (-LRU_C) * r * softplus_neg_lam[:, cols]
        a = jnp.exp(log_a)
        mult = jnp.sqrt(-jnp.tanh(log_a) * (a * a + 1.0))
        a_ref[:, cols] = a
        bx_ref[:, cols] = mult * (ig * v)

    def step(t, h):
        h = a_ref[pl.ds(t, 1), :] * h + bx_ref[pl.ds(t, 1), :]
        h_ref[pl.ds(t, 1), :] = h
        return h

    hcarry_ref[...] = lax.fori_loop(0, t_rows, step, hcarry_ref[...], unroll=8)
    ylru_ref[...] = (h_ref[...] * gelu_ref[...].astype(F32)).astype(BF16)


def _mixer(d_pool, v, act_gelu, w_pool, pool_scale, w_a, b_a, w_i, b_i, lam,
           cast_weights, batch, seq):
    n, pool_width = d_pool.shape
    lru_width = v.shape[1]
    t = MIXER_T
    assert seq % t == 0
    nblk = seq // t
    row = lambda b, i: (b * nblk + i, 0)
    return pl.pallas_call(
        _mixer_kernel,
        grid=(batch, nblk),
        in_specs=[
            pl.BlockSpec((t, pool_width), row),
            pl.BlockSpec((t, lru_width), row),
            pl.BlockSpec((t, lru_width), row),
            _resident(w_pool.shape), _resident(pool_scale.shape),
            _resident(w_a.shape), _resident(b_a.shape),
            _resident(w_i.shape), _resident(b_i.shape), _resident(lam.shape),
            *_slab_specs(cast_weights, batch * nblk, row),
        ],
        out_specs=[pl.BlockSpec((t, pool_width), row), pl.BlockSpec((t, lru_width), row),
                   *_slab_specs(cast_weights, batch * nblk, row)],
        out_shape=[jax.ShapeDtypeStruct((n, pool_width), BF16),
                   jax.ShapeDtypeStruct((n, lru_width), BF16),
                   *[jax.ShapeDtypeStruct(w.shape, BF16) for w in cast_weights]],
        scratch_shapes=[
            pltpu.VMEM((t, lru_width), F32),
            pltpu.VMEM((t, lru_width), F32),
            pltpu.VMEM((t, lru_width), F32),
            pltpu.VMEM((1, lru_width), F32),
        ],
        compiler_params=pltpu.CompilerParams(
            dimension_semantics=("arbitrary", "arbitrary"),
            vmem_limit_bytes=VMEM_LIMIT_BYTES),
    )(d_pool, v, act_gelu, w_pool, pool_scale, w_a, b_a, w_i, b_i, lam, *cast_weights)


def _merge_kernel(yp_ref, yl_ref, gates_ref, x_ref, wpp_ref, wlp_ref, wout_ref, wup_ref, wdn_ref,
                  o_ref, wup_bf_ref, wdn_bf_ref, m_ref):
    _cast_slabs((wup_ref, wdn_ref), (wup_bf_ref, wdn_bf_ref))
    d = o_ref.shape[1]
    ch = MERGE_CHUNK
    for c in range(d // ch):
        cols = slice(c * ch, (c + 1) * ch)
        p = _dot(yp_ref[...], wpp_ref[:, cols])
        q = _dot(yl_ref[...], wlp_ref[:, cols])
        g_pool = gates_ref[:, cols].astype(F32)
        g_lru = gates_ref[:, d + c * ch:d + (c + 1) * ch].astype(F32)
        m_ref[:, cols] = (g_pool * p + g_lru * q).astype(BF16)
    for c in range(d // ch):
        cols = slice(c * ch, (c + 1) * ch)
        o_ref[:, cols] = x_ref[:, cols] + _dot(m_ref[...], wout_ref[:, cols])


def _merge(y_pool, y_lru, gates, x, w_pool_proj, w_lru_proj, w_out, cast_weights):
    n, d = x.shape
    tm = MERGE_TM
    assert n % tm == 0 and d % MERGE_CHUNK == 0
    row = lambda i: (i, 0)
    return pl.pallas_call(
        _merge_kernel,
        grid=(n // tm,),
        in_specs=[
            pl.BlockSpec((tm, y_pool.shape[1]), row),
            pl.BlockSpec((tm, y_lru.shape[1]), row),
            pl.BlockSpec((tm, gates.shape[1]), row),
            pl.BlockSpec((tm, d), row),
            _resident(w_pool_proj.shape), _resident(w_lru_proj.shape), _resident(w_out.shape),
            *_slab_specs(cast_weights, n // tm, row),
        ],
        out_specs=[pl.BlockSpec((tm, d), row), *_slab_specs(cast_weights, n // tm, row)],
        out_shape=[jax.ShapeDtypeStruct((n, d), F32),
                   *[jax.ShapeDtypeStruct(w.shape, BF16) for w in cast_weights]],
        scratch_shapes=[pltpu.VMEM((tm, d), BF16)],
        compiler_params=pltpu.CompilerParams(
            dimension_semantics=("arbitrary",),
            vmem_limit_bytes=VMEM_LIMIT_BYTES),
    )(y_pool, y_lru, gates, x, w_pool_proj, w_lru_proj, w_out, *cast_weights)


def _ffn_kernel(apply_final, x_ref, g_ref, wg_ref, wv_ref, cw_ref, cb_ref, wd_ref, gfin_ref,
                o_ref, h2_ref, acc_ref, ext_ref, carry_ref):
    i = pl.program_id(1)
    f = pl.program_id(2)
    t_rows = x_ref.shape[0]

    @pl.when(f == 0)
    def _():
        h2_ref[...] = _rms_norm(x_ref[...], g_ref[...]).astype(BF16)
        acc_ref[...] = jnp.zeros_like(acc_ref)

    @pl.when(i == 0)
    def _():
        ext_ref[0:CONV_HALO, :] = jnp.zeros((CONV_HALO, ext_ref.shape[1]), F32)

    @pl.when(i > 0)
    def _():
        ext_ref[0:CONV_HALO, :] = carry_ref[f]

    n_taps = cw_ref.shape[0]
    for r in range(t_rows // ROW_SUB):
        rows = slice(r * ROW_SUB, (r + 1) * ROW_SUB)
        base = CONV_HALO + r * ROW_SUB
        ext_ref[base:base + ROW_SUB, :] = _dot(h2_ref[rows, :], wg_ref[...])
        val = _dot(h2_ref[rows, :], wv_ref[...])
        conv = cb_ref[...]
        for k in range(n_taps):
            conv = conv + (ext_ref[base - k:base - k + ROW_SUB, :]
                           * cw_ref[n_taps - 1 - k:n_taps - k, :])
        act = (jax.nn.gelu(conv) * val).astype(BF16)
        acc_ref[rows, :] += _dot(act, wd_ref[...])

    carry_ref[f] = ext_ref[t_rows:t_rows + CONV_HALO, :]

    @pl.when(f == pl.num_programs(2) - 1)
    def _():
        y = x_ref[...] + acc_ref[...]
        if apply_final:
            y = _rms_norm(y, gfin_ref[...])
        o_ref[...] = y


def _ffn(x, g, w_up, conv_w, conv_b, w_down, g_final, apply_final, batch, seq):
    n, d = x.shape
    d_ff = w_down.shape[0]
    t, tf = FFN_T, FFN_TF
    assert seq % t == 0 and d_ff % tf == 0
    nblk, nf = seq // t, d_ff // tf
    row = lambda b, i, f: (b * nblk + i, 0)
    return pl.pallas_call(
        functools.partial(_ffn_kernel, apply_final),
        grid=(batch, nblk, nf),
        in_specs=[
            pl.BlockSpec((t, d), row),
            pl.BlockSpec((1, d), lambda b, i, f: (0, 0)),
            pl.BlockSpec((d, tf), lambda b, i, f: (0, f)),
            pl.BlockSpec((d, tf), lambda b, i, f: (0, nf + f)),
            pl.BlockSpec((conv_w.shape[0], tf), lambda b, i, f: (0, f)),
            pl.BlockSpec((1, tf), lambda b, i, f: (0, f)),
            pl.BlockSpec((tf, d), lambda b, i, f: (f, 0)),
            pl.BlockSpec((1, d), lambda b, i, f: (0, 0)),
        ],
        out_specs=pl.BlockSpec((t, d), row),
        out_shape=jax.ShapeDtypeStruct((n, d), F32),
        scratch_shapes=[
            pltpu.VMEM((t, d), BF16),
            pltpu.VMEM((t, d), F32),
            pltpu.VMEM((CONV_HALO + t, tf), F32),
            pltpu.VMEM((nf, CONV_HALO, tf), F32),
        ],
        compiler_params=pltpu.CompilerParams(
            dimension_semantics=("arbitrary", "arbitrary", "arbitrary"),
            vmem_limit_bytes=VMEM_LIMIT_BYTES),
    )(x, g, w_up, w_up, conv_w, conv_b, w_down, g_final)


def kernel(x, g_mix, w_in, b_gate, w_pool, pool_scale, lru_conv_w, lru_conv_b, w_a, b_a, w_i,
           b_i, lru_lambda, w_pool_proj, w_lru_proj, w_out, g_mlp, w_up, ffn_conv_w, ffn_conv_b,
           w_down, g_final):
    batch, seq, d = x.shape
    depth = w_in.shape[0]
    pool_width = pool_scale.shape[1]
    lru_width = lru_lambda.shape[1]
    row2 = lambda v: v.reshape(1, -1)
    xf = x.reshape(batch * seq, d)
    for l in range(depth):
        d_pool, v, act_gelu, gates = _inproj(
            xf, row2(g_mix[l]), w_in[l].astype(BF16), row2(b_gate[l]),
            lru_conv_w[l], row2(lru_conv_b[l]), pool_width, lru_width, seq)
        y_pool, y_lru, wpp_bf, wlp_bf, wout_bf = _mixer(
            d_pool, v, act_gelu, w_pool[l].astype(BF16), row2(pool_scale[l]),
            w_a[l].astype(BF16), row2(b_a[l]), w_i[l].astype(BF16), row2(b_i[l]),
            row2(lru_lambda[l]), (w_pool_proj[l], w_lru_proj[l], w_out[l]), batch, seq)
        xf, wup_bf, wdn_bf = _merge(y_pool, y_lru, gates, xf, wpp_bf, wlp_bf, wout_bf,
                                    (w_up[l], w_down[l]))
        xf = _ffn(xf, row2(g_mlp[l]), wup_bf, ffn_conv_w[l], row2(ffn_conv_b[l]),
                  wdn_bf, row2(g_final), l == depth - 1, batch, seq)
    return xf.reshape(batch, seq, d)
```

```python
import functools

import jax
import jax.numpy as jnp
from jax import lax
from jax.experimental import pallas as pl
from jax.experimental.pallas import tpu as pltpu

F32 = jnp.float32
BF16 = jnp.bfloat16

POOL_WINDOWS = (2, 4, 8, 16)
POOL_GROUP_WIDTH = 256
LRU_BLOCK_WIDTH = 256
LRU_C = 8.0
EPS = 1e-6

SUBLANES = 8
BF16_SUBLANES = 16
POOL_SUB = 4
POOL_SUB_HALO = 16
SEQ_HALO = 24
CONV_HALO = SUBLANES
VMEM_LIMIT_BYTES = 56 * 1024 * 1024

ROW_SUB = 256
INPROJ_TM = 512
INPROJ_TN = 1024
GATE_TN = 1024
MIXER_T = 512
MERGE_TM = 256
MERGE_CHUNK = 512
FFN_T = 512
FFN_TF = 1024


def _rms_norm(x, g):
    ms = jnp.mean(x * x, axis=-1, keepdims=True)
    return x * lax.rsqrt(ms + EPS) * g


def _dot(a, b):
    return jnp.dot(a, b, preferred_element_type=F32)


def _resident(shape):
    zeros = (0,) * len(shape)
    return pl.BlockSpec(shape, lambda *_: zeros, pipeline_mode=pl.Buffered(1))


def _slab_specs(weights, n_steps, step_index):
    specs = []
    for w in weights:
        rows, cols = w.shape
        assert rows % (n_steps * BF16_SUBLANES) == 0
        specs.append(pl.BlockSpec((rows // n_steps, cols), step_index))
    return specs


def _cast_slabs(src_refs, dst_refs):
    for src, dst in zip(src_refs, dst_refs):
        dst[...] = src[...].astype(BF16)


def _mixproj_kernel(blocks_per_seq, x_ref, g_ref, w_ref, convw_ref, convb_ref,
                    d_ref, v_ref, hout_ref, h_ref, ext_ref, sub_ref, carry_ref):
    tn = INPROJ_TN
    n_sub = x_ref.shape[0] // ROW_SUB
    block_in_seq = lax.rem(pl.program_id(0), blocks_per_seq)

    @pl.when(block_in_seq == 0)
    def _():
        carry_ref[...] = jnp.zeros_like(carry_ref)

    @pl.loop(0, n_sub)
    def _(r):
        rows = pl.ds(pl.multiple_of(r * ROW_SUB, ROW_SUB), ROW_SUB)
        h = _rms_norm(x_ref[rows, :], g_ref[...]).astype(BF16)
        h_ref[rows, :] = h
        hout_ref[rows, :] = h

    def project(tile, r):
        slot = r % 2
        if r == 0:
            ext_ref[slot, 0:SEQ_HALO, :] = carry_ref[tile]
        ext_ref[slot, SEQ_HALO:SEQ_HALO + ROW_SUB, :] = _dot(
            h_ref[r * ROW_SUB:(r + 1) * ROW_SUB, :], w_ref[:, tile * tn:(tile + 1) * tn])
        return slot

    def pass_tail(tile, r):
        tail = ext_ref[r % 2, ROW_SUB:ROW_SUB + SEQ_HALO, :]
        if r + 1 < n_sub:
            ext_ref[(r + 1) % 2, 0:SEQ_HALO, :] = tail
        else:
            carry_ref[tile] = tail

    def window(load, q0, n, count, stride):
        s = load(q0, n)
        for m in range(1, count):
            s = s + load(q0 - m * stride, n)
        return s

    wide = [g for g, w in enumerate(POOL_WINDOWS) if w > POOL_SUB]
    for r in range(n_sub):
        rows = slice(r * ROW_SUB, (r + 1) * ROW_SUB)
        slot = project(0, r)
        pos = (block_in_seq * x_ref.shape[0] + r * ROW_SUB + 1
               + lax.broadcasted_iota(jnp.int32, (ROW_SUB, 1), 0)).astype(F32)
        for g, w in enumerate(POOL_WINDOWS):
            cols = slice(g * POOL_GROUP_WIDTH, (g + 1) * POOL_GROUP_WIDTH)
            load_ext = lambda q0, n: ext_ref[slot, q0:q0 + n, cols]
            if w <= POOL_SUB:
                s = window(load_ext, SEQ_HALO, ROW_SUB, w, 1)
            else:
                scols = slice(wide.index(g) * POOL_GROUP_WIDTH,
                              (wide.index(g) + 1) * POOL_GROUP_WIDTH)
                n = POOL_SUB_HALO + ROW_SUB
                sub_ref[slot, 0:n, scols] = window(
                    load_ext, SEQ_HALO - POOL_SUB_HALO, n, POOL_SUB, 1)
                load_sub = lambda q0, n: sub_ref[slot, q0:q0 + n, scols]
                s = window(load_sub, POOL_SUB_HALO, ROW_SUB, w // POOL_SUB, POOL_SUB)
            inv_count = 1.0 / jnp.minimum(pos, float(w))
            d = s * inv_count - ext_ref[slot, SEQ_HALO:SEQ_HALO + ROW_SUB, cols]
            d_ref[rows, cols] = d.astype(BF16)
        pass_tail(0, r)

    n_taps = convw_ref.shape[0]
    for tile in range(1, w_ref.shape[1] // tn):
        cols = slice((tile - 1) * tn, tile * tn)
        for r in range(n_sub):
            slot = project(tile, r)
            v = convb_ref[:, cols]
            for k in range(n_taps):
                v = v + (ext_ref[slot, SEQ_HALO - k:SEQ_HALO - k + ROW_SUB, :]
                         * convw_ref[n_taps - 1 - k:n_taps - k, cols])
            v_ref[r * ROW_SUB:(r + 1) * ROW_SUB, cols] = v
            pass_tail(tile, r)


def _mixproj(x, g, w_in, conv_w, conv_b, pool_width, mix_width, seq):
    n, d = x.shape
    assert w_in.shape[1] % mix_width == 0
    lru_width = mix_width - pool_width
    tm, tn = INPROJ_TM, INPROJ_TN
    assert pool_width == tn == len(POOL_WINDOWS) * POOL_GROUP_WIDTH and lru_width % tn == 0
    assert n % tm == 0 and seq % tm == 0 and tm % ROW_SUB == 0
    n_wide = sum(w > POOL_SUB for w in POOL_WINDOWS)
    row = lambda i: (i, 0)
    return pl.pallas_call(
        functools.partial(_mixproj_kernel, seq // tm),
        grid=(n // tm,),
        in_specs=[
            pl.BlockSpec((tm, d), row),
            _resident((1, d)),
            pl.BlockSpec((d, mix_width), lambda i: (0, 0), pipeline_mode=pl.Buffered(1)),
            _resident(conv_w.shape), _resident(conv_b.shape),
        ],
        out_specs=[pl.BlockSpec((tm, pool_width), row), pl.BlockSpec((tm, lru_width), row),
                   pl.BlockSpec((tm, d), row)],
        out_shape=[
            jax.ShapeDtypeStruct((n, pool_width), BF16),
            jax.ShapeDtypeStruct((n, lru_width), F32),
            jax.ShapeDtypeStruct((n, d), BF16),
        ],
        scratch_shapes=[
            pltpu.VMEM((tm, d), BF16),
            pltpu.VMEM((2, SEQ_HALO + ROW_SUB, tn), F32),
            pltpu.VMEM((2, POOL_SUB_HALO + ROW_SUB, n_wide * POOL_GROUP_WIDTH), F32),
            pltpu.VMEM((mix_width // tn, SEQ_HALO, tn), F32),
        ],
        compiler_params=pltpu.CompilerParams(
            dimension_semantics=("arbitrary",),
            vmem_limit_bytes=VMEM_LIMIT_BYTES),
    )(x, g, w_in, conv_w, conv_b)


def _gateproj_kernel(n_gelu, n_wblocks, h_ref, *refs):
    w_refs = refs[:n_wblocks]
    b_ref, gelu_ref, gates_ref = refs[n_wblocks:]
    tn = GATE_TN
    tiles_per_block = w_refs[0].shape[1] // tn
    for tile in range(n_wblocks * tiles_per_block):
        w_ref = w_refs[tile // tiles_per_block]
        wcols = slice((tile % tiles_per_block) * tn, (tile % tiles_per_block + 1) * tn)
        for r in range(h_ref.shape[0] // ROW_SUB):
            rows = slice(r * ROW_SUB, (r + 1) * ROW_SUB)
            p = _dot(h_ref[rows, :], w_ref[:, wcols])
            if tile < n_gelu:
                gelu_ref[rows, tile * tn:(tile + 1) * tn] = jax.nn.gelu(p).astype(BF16)
            else:
                cols = slice((tile - n_gelu) * tn, (tile - n_gelu + 1) * tn)
                gates_ref[rows, cols] = jax.nn.sigmoid(p + b_ref[:, cols]).astype(BF16)


def _gateproj(h, w_in, b_gate, mix_width, gelu_width):
    n, d = h.shape
    gate_width = w_in.shape[1] - mix_width - gelu_width
    tm, tn = INPROJ_TM, GATE_TN
    assert n % tm == 0 and gelu_width % tn == 0 and gate_width % tn == 0
    assert w_in.shape[1] % mix_width == 0 and mix_width % tn == 0
    n_wblocks = w_in.shape[1] // mix_width - 1
    row = lambda i: (i, 0)
    w_specs = [pl.BlockSpec((d, mix_width), functools.partial(lambda k, i: (0, k), 1 + k),
                            pipeline_mode=pl.Buffered(1)) for k in range(n_wblocks)]
    return pl.pallas_call(
        functools.partial(_gateproj_kernel, gelu_width // tn, n_wblocks),
        grid=(n // tm,),
        in_specs=[pl.BlockSpec((tm, d), row), *w_specs, _resident(b_gate.shape)],
        out_specs=[pl.BlockSpec((tm, gelu_width), row), pl.BlockSpec((tm, gate_width), row)],
        out_shape=[jax.ShapeDtypeStruct((n, gelu_width), BF16),
                   jax.ShapeDtypeStruct((n, gate_width), BF16)],
        compiler_params=pltpu.CompilerParams(
            dimension_semantics=("arbitrary",),
            vmem_limit_bytes=VMEM_LIMIT_BYTES),
    )(h, *([w_in] * n_wblocks), b_gate)


def _mixer_kernel(d_ref, v_ref, gelu_ref, wpool_ref, pscale_ref,
                  wa_ref, ba_ref, wi_ref, bi_ref, lam_ref, wpp_ref, wlp_ref, wout_ref,
                  ypool_ref, ylru_ref, wpp_bf_ref, wlp_bf_ref, wout_bf_ref,
                  a_ref, bx_ref, h_ref, hcarry_ref):
    _cast_slabs((wpp_ref, wlp_ref, wout_ref), (wpp_bf_ref, wlp_bf_ref, wout_bf_ref))

    t_rows = ylru_ref.shape[0]
    lru_width = ylru_ref.shape[1]

    @pl.when(pl.program_id(1) == 0)
    def _():
        hcarry_ref[...] = jnp.zeros_like(hcarry_ref)

    for g in range(len(POOL_WINDOWS)):
        cols = slice(g * POOL_GROUP_WIDTH, (g + 1) * POOL_GROUP_WIDTH)
        y = _dot(d_ref[:, cols], wpool_ref[g]) * pscale_ref[:, cols]
        ypool_ref[:, cols] = y.astype(BF16)

    softplus_neg_lam = jax.nn.softplus(-lam_ref[...])
    for hd in range(lru_width // LRU_BLOCK_WIDTH):
        cols = slice(hd * LRU_BLOCK_WIDTH, (hd + 1) * LRU_BLOCK_WIDTH)
        v = v_ref[:, cols]
        vb = v.astype(BF16)
        r = jax.nn.sigmoid(_dot(vb, wa_ref[hd]) + ba_ref[:, cols])
        ig = jax.nn.sigmoid(_dot(vb, wi_ref[hd]) + bi_ref[:, cols])
        log_a = (-LRU_C) * r * softplus_neg_lam[:, cols]
        a = jnp.exp(log_a)
        mult = jnp.sqrt(-jnp.tanh(log_a) * (a * a + 1.0))
        a_ref[:, cols] = a
        bx_ref[:, cols] = mult * (ig * v)

    def step(t, h):
        h = a_ref[pl.ds(t, 1), :] * h + bx_ref[pl.ds(t, 1), :]
        h_ref[pl.ds(t, 1), :] = h
        return h

    hcarry_ref[...] = lax.fori_loop(0, t_rows, step, hcarry_ref[...], unroll=8)
    ylru_ref[...] = (h_ref[...] * gelu_ref[...].astype(F32)).astype(BF16)


def _mixer(d_pool, v, act_gelu, w_pool, pool_scale, w_a, b_a, w_i, b_i, lam,
           cast_weights, batch, seq):
    n, pool_width = d_pool.shape
    lru_width = v.shape[1]
    t = MIXER_T
    assert seq % t == 0
    nblk = seq // t
    row = lambda b, i: (b * nblk + i, 0)
    return pl.pallas_call(
        _mixer_kernel,
        grid=(batch, nblk),
        in_specs=[
            pl.BlockSpec((t, pool_width), row),
            pl.BlockSpec((t, lru_width), row),
            pl.BlockSpec((t, lru_width), row),
            _resident(w_pool.shape), _resident(pool_scale.shape),
            _resident(w_a.shape), _resident(b_a.shape),
            _resident(w_i.shape), _resident(b_i.shape), _resident(lam.shape),
            *_slab_specs(cast_weights, batch * nblk, row),
        ],
        out_specs=[pl.BlockSpec((t, pool_width), row), pl.BlockSpec((t, lru_width), row),
                   *_slab_specs(cast_weights, batch * nblk, row)],
        out_shape=[jax.ShapeDtypeStruct((n, pool_width), BF16),
                   jax.ShapeDtypeStruct((n, lru_width), BF16),
                   *[jax.ShapeDtypeStruct(w.shape, BF16) for w in cast_weights]],
        scratch_shapes=[
            pltpu.VMEM((t, lru_width), F32),
            pltpu.VMEM((t, lru_width), F32),
            pltpu.VMEM((t, lru_width), F32),
            pltpu.VMEM((1, lru_width), F32),
        ],
        compiler_params=pltpu.CompilerParams(
            dimension_semantics=("arbitrary", "arbitrary"),
            vmem_limit_bytes=VMEM_LIMIT_BYTES),
    )(d_pool, v, act_gelu, w_pool, pool_scale, w_a, b_a, w_i, b_i, lam, *cast_weights)


def _merge_kernel(yp_ref, yl_ref, gates_ref, x_ref, wpp_ref, wlp_ref, wout_ref, wup_ref, wdn_ref,
                  o_ref, wup_bf_ref, wdn_bf_ref, m_ref):
    _cast_slabs((wup_ref, wdn_ref), (wup_bf_ref, wdn_bf_ref))
    d = o_ref.shape[1]
    ch = MERGE_CHUNK
    for c in range(d // ch):
        cols = slice(c * ch, (c + 1) * ch)
        p = _dot(yp_ref[...], wpp_ref[:, cols])
        q = _dot(yl_ref[...], wlp_ref[:, cols])
        g_pool = gates_ref[:, cols].astype(F32)
        g_lru = gates_ref[:, d + c * ch:d + (c + 1) * ch].astype(F32)
        m_ref[:, cols] = (g_pool * p + g_lru * q).astype(BF16)
    for c in range(d // ch):
        cols = slice(c * ch, (c + 1) * ch)
        o_ref[:, cols] = x_ref[:, cols] + _dot(m_ref[...], wout_ref[:, cols])


def _merge(y_pool, y_lru, gates, x, w_pool_proj, w_lru_proj, w_out, cast_weights):
    n, d = x.shape
    tm = MERGE_TM
    assert n % tm == 0 and d % MERGE_CHUNK == 0
    row = lambda i: (i, 0)
    return pl.pallas_call(
        _merge_kernel,
        grid=(n // tm,),
        in_specs=[
            pl.BlockSpec((tm, y_pool.shape[1]), row),
            pl.BlockSpec((tm, y_lru.shape[1]), row),
            pl.BlockSpec((tm, gates.shape[1]), row),
            pl.BlockSpec((tm, d), row),
            _resident(w_pool_proj.shape), _resident(w_lru_proj.shape), _resident(w_out.shape),
            *_slab_specs(cast_weights, n // tm, row),
        ],
        out_specs=[pl.BlockSpec((tm, d), row), *_slab_specs(cast_weights, n // tm, row)],
        out_shape=[jax.ShapeDtypeStruct((n, d), F32),
                   *[jax.ShapeDtypeStruct(w.shape, BF16) for w in cast_weights]],
        scratch_shapes=[pltpu.VMEM((tm, d), BF16)],
        compiler_params=pltpu.CompilerParams(
            dimension_semantics=("arbitrary",),
            vmem_limit_bytes=VMEM_LIMIT_BYTES),
    )(y_pool, y_lru, gates, x, w_pool_proj, w_lru_proj, w_out, *cast_weights)


def _ffn_kernel(apply_final, x_ref, g_ref, wg_ref, wv_ref, cw_ref, cb_ref, wd_ref, gfin_ref,
                o_ref, h2_ref, acc_ref, ext_ref, carry_ref):
    i = pl.program_id(1)
    f = pl.program_id(2)
    t_rows = x_ref.shape[0]

    @pl.when(f == 0)
    def _():
        h2_ref[...] = _rms_norm(x_ref[...], g_ref[...]).astype(BF16)
        acc_ref[...] = jnp.zeros_like(acc_ref)

    @pl.when(i == 0)
    def _():
        ext_ref[0:CONV_HALO, :] = jnp.zeros((CONV_HALO, ext_ref.shape[1]), F32)

    @pl.when(i > 0)
    def _():
        ext_ref[0:CONV_HALO, :] = carry_ref[f]

    n_taps = cw_ref.shape[0]
    for r in range(t_rows // ROW_SUB):
        rows = slice(r * ROW_SUB, (r + 1) * ROW_SUB)
        base = CONV_HALO + r * ROW_SUB
        ext_ref[base:base + ROW_SUB, :] = _dot(h2_ref[rows, :], wg_ref[...])
        val = _dot(h2_ref[rows, :], wv_ref[...])
        conv = cb_ref[...]
        for k in range(n_taps):
            conv = conv + (ext_ref[base - k:base - k + ROW_SUB, :]
                           * cw_ref[n_taps - 1 - k:n_taps - k, :])
        act = (jax.nn.gelu(conv) * val).astype(BF16)
        acc_ref[rows, :] += _dot(act, wd_ref[...])

    carry_ref[f] = ext_ref[t_rows:t_rows + CONV_HALO, :]

    @pl.when(f == pl.num_programs(2) - 1)
    def _():
        y = x_ref[...] + acc_ref[...]
        if apply_final:
            y = _rms_norm(y, gfin_ref[...])
        o_ref[...] = y


def _ffn(x, g, w_up, conv_w, conv_b, w_down, g_final, apply_final, batch, seq):
    n, d = x.shape
    d_ff = w_down.shape[0]
    t, tf = FFN_T, FFN_TF
    assert seq % t == 0 and d_ff % tf == 0
    nblk, nf = seq // t, d_ff // tf
    row = lambda b, i, f: (b * nblk + i, 0)
    return pl.pallas_call(
        functools.partial(_ffn_kernel, apply_final),
        grid=(batch, nblk, nf),
        in_specs=[
            pl.BlockSpec((t, d), row),
            pl.BlockSpec((1, d), lambda b, i, f: (0, 0)),
            pl.BlockSpec((d, tf), lambda b, i, f: (0, f)),
            pl.BlockSpec((d, tf), lambda b, i, f: (0, nf + f)),
            pl.BlockSpec((conv_w.shape[0], tf), lambda b, i, f: (0, f)),
            pl.BlockSpec((1, tf), lambda b, i, f: (0, f)),
            pl.BlockSpec((tf, d), lambda b, i, f: (f, 0)),
            pl.BlockSpec((1, d), lambda b, i, f: (0, 0)),
        ],
        out_specs=pl.BlockSpec((t, d), row),
        out_shape=jax.ShapeDtypeStruct((n, d), F32),
        scratch_shapes=[
            pltpu.VMEM((t, d), BF16),
            pltpu.VMEM((t, d), F32),
            pltpu.VMEM((CONV_HALO + t, tf), F32),
            pltpu.VMEM((nf, CONV_HALO, tf), F32),
        ],
        compiler_params=pltpu.CompilerParams(
            dimension_semantics=("arbitrary", "arbitrary", "arbitrary"),
            vmem_limit_bytes=VMEM_LIMIT_BYTES),
    )(x, g, w_up, w_up, conv_w, conv_b, w_down, g_final)


def kernel(x, g_mix, w_in, b_gate, w_pool, pool_scale, lru_conv_w, lru_conv_b, w_a, b_a, w_i,
           b_i, lru_lambda, w_pool_proj, w_lru_proj, w_out, g_mlp, w_up, ffn_conv_w, ffn_conv_b,
           w_down, g_final):
    batch, seq, d = x.shape
    depth = w_in.shape[0]
    pool_width = pool_scale.shape[1]
    lru_width = lru_lambda.shape[1]
    row2 = lambda v: v.reshape(1, -1)
    xf = x.reshape(batch * seq, d)
    for l in range(depth):
        mix_width = pool_width + lru_width
        w_in_bf = w_in[l].astype(BF16)
        d_pool, v, h = _mixproj(xf, row2(g_mix[l]), w_in_bf, lru_conv_w[l],
                                row2(lru_conv_b[l]), pool_width, mix_width, seq)
        act_gelu, gates = _gateproj(h, w_in_bf, row2(b_gate[l]), mix_width, lru_width)
        y_pool, y_lru, wpp_bf, wlp_bf, wout_bf = _mixer(
            d_pool, v, act_gelu, w_pool[l].astype(BF16), row2(pool_scale[l]),
            w_a[l].astype(BF16), row2(b_a[l]), w_i[l].astype(BF16), row2(b_i[l]),
            row2(lru_lambda[l]), (w_pool_proj[l], w_lru_proj[l], w_out[l]), batch, seq)
        xf, wup_bf, wdn_bf = _merge(y_pool, y_lru, gates, xf, wpp_bf, wlp_bf, wout_bf,
                                    (w_up[l], w_down[l]))
        xf = _ffn(xf, row2(g_mlp[l]), wup_bf, ffn_conv_w[l], row2(ffn_conv_b[l]),
                  wdn_bf, row2(g_final), l == depth - 1, batch, seq)
    return xf.reshape(batch, seq, d)
```

```python
import functools

import jax
import jax.numpy as jnp
from jax import lax
from jax.experimental import pallas as pl
from jax.experimental.pallas import tpu as pltpu

F32 = jnp.float32
BF16 = jnp.bfloat16

POOL_WINDOWS = (2, 4, 8, 16)
POOL_GROUP_WIDTH = 256
LRU_BLOCK_WIDTH = 256
LRU_C = 8.0
EPS = 1e-6

SUBLANES = 8
BF16_SUBLANES = 16
POOL_SUB = 4
POOL_SUB_HALO = 16
SEQ_HALO = 24
CONV_HALO = SUBLANES
VMEM_LIMIT_BYTES = 56 * 1024 * 1024

ROW_SUB = 256
PROJ_TN = 1024
MIXPROJ_TM = 512
GATEPROJ_TM = 512
MIXER_T = 512
MERGE_TM = 512
MERGE_CHUNK = 512
FFN_T = 512
FFN_TF = 1024


def _rms_norm(x, g):
    ms = jnp.mean(x * x, axis=-1, keepdims=True)
    return x * lax.rsqrt(ms + EPS) * g


def _dot(a, b):
    return jnp.dot(a, b, preferred_element_type=F32)


def _sigmoid(x):
    return 0.5 * jnp.tanh(0.5 * x) + 0.5


def _resident(shape):
    zeros = (0,) * len(shape)
    return pl.BlockSpec(shape, lambda *_: zeros, pipeline_mode=pl.Buffered(1))


def _slab_specs(weights, n_steps, step_index):
    specs = []
    for w in weights:
        rows, cols = w.shape
        assert rows % (n_steps * BF16_SUBLANES) == 0
        specs.append(pl.BlockSpec((rows // n_steps, cols), step_index))
    return specs


def _cast_slabs(src_refs, dst_refs):
    for src, dst in zip(src_refs, dst_refs):
        dst[...] = src[...].astype(BF16)


def _mixproj_kernel(blocks_per_seq, x_ref, g_ref, w_ref, convw_ref, convb_ref, cast_ref,
                    d_ref, v_ref, hout_ref, cast_bf_ref, h_ref, ext_ref, sub_ref, carry_ref):
    _cast_slabs((cast_ref,), (cast_bf_ref,))
    tn = PROJ_TN
    n_sub = x_ref.shape[0] // ROW_SUB
    block_in_seq = lax.rem(pl.program_id(0), blocks_per_seq)

    @pl.when(block_in_seq == 0)
    def _():
        carry_ref[...] = jnp.zeros_like(carry_ref)

    @pl.loop(0, n_sub)
    def _(r):
        rows = pl.ds(pl.multiple_of(r * ROW_SUB, ROW_SUB), ROW_SUB)
        h = _rms_norm(x_ref[rows, :], g_ref[...]).astype(BF16)
        h_ref[rows, :] = h
        hout_ref[rows, :] = h

    def project(tile, r):
        slot = r % 2
        if r == 0:
            ext_ref[slot, 0:SEQ_HALO, :] = carry_ref[tile]
        ext_ref[slot, SEQ_HALO:SEQ_HALO + ROW_SUB, :] = _dot(
            h_ref[r * ROW_SUB:(r + 1) * ROW_SUB, :], w_ref[:, tile * tn:(tile + 1) * tn])
        return slot

    def pass_tail(tile, r):
        tail = ext_ref[r % 2, ROW_SUB:ROW_SUB + SEQ_HALO, :]
        if r + 1 < n_sub:
            ext_ref[(r + 1) % 2, 0:SEQ_HALO, :] = tail
        else:
            carry_ref[tile] = tail

    def window(load, q0, n, count, stride):
        s = load(q0, n)
        for m in range(1, count):
            s = s + load(q0 - m * stride, n)
        return s

    wide = [g for g, w in enumerate(POOL_WINDOWS) if w > POOL_SUB]
    for r in range(n_sub):
        rows = slice(r * ROW_SUB, (r + 1) * ROW_SUB)
        slot = project(0, r)
        pos = (block_in_seq * x_ref.shape[0] + r * ROW_SUB + 1
               + lax.broadcasted_iota(jnp.int32, (ROW_SUB, 1), 0)).astype(F32)
        for g, w in enumerate(POOL_WINDOWS):
            cols = slice(g * POOL_GROUP_WIDTH, (g + 1) * POOL_GROUP_WIDTH)
            load_ext = lambda q0, n: ext_ref[slot, q0:q0 + n, cols]
            if w <= POOL_SUB:
                s = window(load_ext, SEQ_HALO, ROW_SUB, w, 1)
            else:
                scols = slice(wide.index(g) * POOL_GROUP_WIDTH,
                              (wide.index(g) + 1) * POOL_GROUP_WIDTH)
                n = POOL_SUB_HALO + ROW_SUB
                sub_ref[slot, 0:n, scols] = window(
                    load_ext, SEQ_HALO - POOL_SUB_HALO, n, POOL_SUB, 1)
                load_sub = lambda q0, n: sub_ref[slot, q0:q0 + n, scols]
                s = window(load_sub, POOL_SUB_HALO, ROW_SUB, w // POOL_SUB, POOL_SUB)
            inv_count = 1.0 / jnp.minimum(pos, float(w))
            d = s * inv_count - ext_ref[slot, SEQ_HALO:SEQ_HALO + ROW_SUB, cols]
            d_ref[rows, cols] = d.astype(BF16)
        pass_tail(0, r)

    n_taps = convw_ref.shape[0]
    for tile in range(1, w_ref.shape[1] // tn):
        cols = slice((tile - 1) * tn, tile * tn)
        for r in range(n_sub):
            slot = project(tile, r)
            v = convb_ref[:, cols]
            for k in range(n_taps):
                v = v + (ext_ref[slot, SEQ_HALO - k:SEQ_HALO - k + ROW_SUB, :]
                         * convw_ref[n_taps - 1 - k:n_taps - k, cols])
            v_ref[r * ROW_SUB:(r + 1) * ROW_SUB, cols] = v
            pass_tail(tile, r)


def _mixproj(x, g, w_in, conv_w, conv_b, cast_weight, pool_width, mix_width, seq):
    n, d = x.shape
    assert w_in.shape[1] % mix_width == 0
    lru_width = mix_width - pool_width
    tm, tn = MIXPROJ_TM, PROJ_TN
    assert pool_width == tn == len(POOL_WINDOWS) * POOL_GROUP_WIDTH and lru_width % tn == 0
    assert n % tm == 0 and seq % tm == 0 and tm % ROW_SUB == 0
    n_wide = sum(w > POOL_SUB for w in POOL_WINDOWS)
    row = lambda i: (i, 0)
    return pl.pallas_call(
        functools.partial(_mixproj_kernel, seq // tm),
        grid=(n // tm,),
        in_specs=[
            pl.BlockSpec((tm, d), row),
            _resident((1, d)),
            pl.BlockSpec((d, mix_width), lambda i: (0, 0), pipeline_mode=pl.Buffered(1)),
            _resident(conv_w.shape), _resident(conv_b.shape),
            *_slab_specs((cast_weight,), n // tm, row),
        ],
        out_specs=[pl.BlockSpec((tm, pool_width), row), pl.BlockSpec((tm, lru_width), row),
                   pl.BlockSpec((tm, d), row), *_slab_specs((cast_weight,), n // tm, row)],
        out_shape=[
            jax.ShapeDtypeStruct((n, pool_width), BF16),
            jax.ShapeDtypeStruct((n, lru_width), F32),
            jax.ShapeDtypeStruct((n, d), BF16),
            jax.ShapeDtypeStruct(cast_weight.shape, BF16),
        ],
        scratch_shapes=[
            pltpu.VMEM((tm, d), BF16),
            pltpu.VMEM((2, SEQ_HALO + ROW_SUB, tn), F32),
            pltpu.VMEM((2, POOL_SUB_HALO + ROW_SUB, n_wide * POOL_GROUP_WIDTH), F32),
            pltpu.VMEM((mix_width // tn, SEQ_HALO, tn), F32),
        ],
        compiler_params=pltpu.CompilerParams(
            dimension_semantics=("arbitrary",),
            vmem_limit_bytes=VMEM_LIMIT_BYTES),
    )(x, g, w_in, conv_w, conv_b, cast_weight)


def _gateproj_kernel(n_gelu, n_wblocks, h_ref, *refs):
    w_refs = refs[:n_wblocks]
    b_ref, cast_ref, gelu_ref, gates_ref, cast_bf_ref = refs[n_wblocks:]
    _cast_slabs((cast_ref,), (cast_bf_ref,))
    tn = PROJ_TN
    tiles_per_block = w_refs[0].shape[1] // tn
    for tile in range(n_wblocks * tiles_per_block):
        w_ref = w_refs[tile // tiles_per_block]
        wcols = slice((tile % tiles_per_block) * tn, (tile % tiles_per_block + 1) * tn)
        for r in range(h_ref.shape[0] // ROW_SUB):
            rows = slice(r * ROW_SUB, (r + 1) * ROW_SUB)
            p = _dot(h_ref[rows, :], w_ref[:, wcols])
            if tile < n_gelu:
                gelu_ref[rows, tile * tn:(tile + 1) * tn] = jax.nn.gelu(p).astype(BF16)
            else:
                cols = slice((tile - n_gelu) * tn, (tile - n_gelu + 1) * tn)
                gates_ref[rows, cols] = jax.nn.sigmoid(p + b_ref[:, cols]).astype(BF16)


def _gateproj(h, w_in, b_gate, cast_weight, mix_width, gelu_width):
    n, d = h.shape
    gate_width = w_in.shape[1] - mix_width - gelu_width
    tm, tn = GATEPROJ_TM, PROJ_TN
    assert n % tm == 0 and gelu_width % tn == 0 and gate_width % tn == 0
    assert w_in.shape[1] % mix_width == 0 and mix_width % tn == 0
    n_wblocks = w_in.shape[1] // mix_width - 1
    row = lambda i: (i, 0)
    w_specs = [pl.BlockSpec((d, mix_width), functools.partial(lambda k, i: (0, k), 1 + k),
                            pipeline_mode=pl.Buffered(1)) for k in range(n_wblocks)]
    return pl.pallas_call(
        functools.partial(_gateproj_kernel, gelu_width // tn, n_wblocks),
        grid=(n // tm,),
        in_specs=[pl.BlockSpec((tm, d), row), *w_specs, _resident(b_gate.shape),
                  *_slab_specs((cast_weight,), n // tm, row)],
        out_specs=[pl.BlockSpec((tm, gelu_width), row), pl.BlockSpec((tm, gate_width), row),
                   *_slab_specs((cast_weight,), n // tm, row)],
        out_shape=[jax.ShapeDtypeStruct((n, gelu_width), BF16),
                   jax.ShapeDtypeStruct((n, gate_width), BF16),
                   jax.ShapeDtypeStruct(cast_weight.shape, BF16)],
        compiler_params=pltpu.CompilerParams(
            dimension_semantics=("arbitrary",),
            vmem_limit_bytes=VMEM_LIMIT_BYTES),
    )(h, *([w_in] * n_wblocks), b_gate, cast_weight)


def _mixer_kernel(d_ref, v_ref, gelu_ref, wpool_ref, pscale_ref,
                  wa_ref, ba_ref, wi_ref, bi_ref, lam_ref, wpp_ref, wlp_ref, wout_ref,
                  ypool_ref, ylru_ref, wpp_bf_ref, wlp_bf_ref, wout_bf_ref,
                  a_ref, bx_ref, h_ref, hcarry_ref):
    _cast_slabs((wpp_ref, wlp_ref, wout_ref), (wpp_bf_ref, wlp_bf_ref, wout_bf_ref))

    t_rows = ylru_ref.shape[0]
    lru_width = ylru_ref.shape[1]

    @pl.when(pl.program_id(1) == 0)
    def _():
        hcarry_ref[...] = jnp.zeros_like(hcarry_ref)

    for g in range(len(POOL_WINDOWS)):
        cols = slice(g * POOL_GROUP_WIDTH, (g + 1) * POOL_GROUP_WIDTH)
        y = _dot(d_ref[:, cols], wpool_ref[g]) * pscale_ref[:, cols]
        ypool_ref[:, cols] = y.astype(BF16)

    softplus_neg_lam = jax.nn.softplus(-lam_ref[...])
    for hd in range(lru_width // LRU_BLOCK_WIDTH):
        cols = slice(hd * LRU_BLOCK_WIDTH, (hd + 1) * LRU_BLOCK_WIDTH)
        v = v_ref[:, cols]
        vb = v.astype(BF16)
        r = _sigmoid(_dot(vb, wa_ref[hd]) + ba_ref[:, cols])
        ig = _sigmoid(_dot(vb, wi_ref[hd]) + bi_ref[:, cols])
        log_a = (-LRU_C) * r * softplus_neg_lam[:, cols]
        a = jnp.exp(log_a)
        q = -jnp.tanh(log_a) * (a * a + 1.0)
        mult = jnp.where(q > 0.0, q * lax.rsqrt(q), 0.0)
        a_ref[:, cols] = a
        bx_ref[:, cols] = mult * (ig * v)

    def step(t, h):
        h = a_ref[pl.ds(t, 1), :] * h + bx_ref[pl.ds(t, 1), :]
        h_ref[pl.ds(t, 1), :] = h
        return h

    hcarry_ref[...] = lax.fori_loop(0, t_rows, step, hcarry_ref[...], unroll=8)
    ylru_ref[...] = (h_ref[...] * gelu_ref[...].astype(F32)).astype(BF16)


def _mixer(d_pool, v, act_gelu, w_pool, pool_scale, w_a, b_a, w_i, b_i, lam,
           cast_weights, batch, seq):
    n, pool_width = d_pool.shape
    lru_width = v.shape[1]
    t = MIXER_T
    assert seq % t == 0
    nblk = seq // t
    row = lambda b, i: (b * nblk + i, 0)
    return pl.pallas_call(
        _mixer_kernel,
        grid=(batch, nblk),
        in_specs=[
            pl.BlockSpec((t, pool_width), row),
            pl.BlockSpec((t, lru_width), row),
            pl.BlockSpec((t, lru_width), row),
            _resident(w_pool.shape), _resident(pool_scale.shape),
            _resident(w_a.shape), _resident(b_a.shape),
            _resident(w_i.shape), _resident(b_i.shape), _resident(lam.shape),
            *_slab_specs(cast_weights, batch * nblk, row),
        ],
        out_specs=[pl.BlockSpec((t, pool_width), row), pl.BlockSpec((t, lru_width), row),
                   *_slab_specs(cast_weights, batch * nblk, row)],
        out_shape=[jax.ShapeDtypeStruct((n, pool_width), BF16),
                   jax.ShapeDtypeStruct((n, lru_width), BF16),
                   *[jax.ShapeDtypeStruct(w.shape, BF16) for w in cast_weights]],
        scratch_shapes=[
            pltpu.VMEM((t, lru_width), F32),
            pltpu.VMEM((t, lru_width), F32),
            pltpu.VMEM((t, lru_width), F32),
            pltpu.VMEM((1, lru_width), F32),
        ],
        compiler_params=pltpu.CompilerParams(
            dimension_semantics=("arbitrary", "arbitrary"),
            vmem_limit_bytes=VMEM_LIMIT_BYTES),
    )(d_pool, v, act_gelu, w_pool, pool_scale, w_a, b_a, w_i, b_i, lam, *cast_weights)


def _merge_kernel(yp_ref, yl_ref, gates_ref, x_ref, wpp_ref, wlp_ref, wout_ref, o_ref, m_ref):
    d = o_ref.shape[1]
    ch = MERGE_CHUNK
    for c in range(d // ch):
        cols = slice(c * ch, (c + 1) * ch)
        p = _dot(yp_ref[...], wpp_ref[:, cols])
        q = _dot(yl_ref[...], wlp_ref[:, cols])
        g_pool = gates_ref[:, cols].astype(F32)
        g_lru = gates_ref[:, d + c * ch:d + (c + 1) * ch].astype(F32)
        m_ref[:, cols] = (g_pool * p + g_lru * q).astype(BF16)
    for c in range(d // ch):
        cols = slice(c * ch, (c + 1) * ch)
        o_ref[:, cols] = x_ref[:, cols] + _dot(m_ref[...], wout_ref[:, cols])


def _merge(y_pool, y_lru, gates, x, w_pool_proj, w_lru_proj, w_out):
    n, d = x.shape
    tm = MERGE_TM
    assert n % tm == 0 and d % MERGE_CHUNK == 0
    row = lambda i: (i, 0)
    return pl.pallas_call(
        _merge_kernel,
        grid=(n // tm,),
        in_specs=[
            pl.BlockSpec((tm, y_pool.shape[1]), row),
            pl.BlockSpec((tm, y_lru.shape[1]), row),
            pl.BlockSpec((tm, gates.shape[1]), row),
            pl.BlockSpec((tm, d), row),
            _resident(w_pool_proj.shape), _resident(w_lru_proj.shape), _resident(w_out.shape),
        ],
        out_specs=pl.BlockSpec((tm, d), row),
        out_shape=jax.ShapeDtypeStruct((n, d), F32),
        scratch_shapes=[pltpu.VMEM((tm, d), BF16)],
        compiler_params=pltpu.CompilerParams(
            dimension_semantics=("arbitrary",),
            vmem_limit_bytes=VMEM_LIMIT_BYTES),
    )(y_pool, y_lru, gates, x, w_pool_proj, w_lru_proj, w_out)


def _ffn_kernel(apply_final, x_ref, g_ref, wg_ref, wv_ref, cw_ref, cb_ref, wd_ref, gfin_ref,
                o_ref, h2_ref, acc_ref, ext_ref, carry_ref):
    i = pl.program_id(1)
    f = pl.program_id(2)
    t_rows = x_ref.shape[0]

    @pl.when(f == 0)
    def _():
        h2_ref[...] = _rms_norm(x_ref[...], g_ref[...]).astype(BF16)
        acc_ref[...] = jnp.zeros_like(acc_ref)

    @pl.when(i == 0)
    def _():
        ext_ref[0:CONV_HALO, :] = jnp.zeros((CONV_HALO, ext_ref.shape[1]), F32)

    @pl.when(i > 0)
    def _():
        ext_ref[0:CONV_HALO, :] = carry_ref[f]

    n_taps = cw_ref.shape[0]
    for r in range(t_rows // ROW_SUB):
        rows = slice(r * ROW_SUB, (r + 1) * ROW_SUB)
        base = CONV_HALO + r * ROW_SUB
        ext_ref[base:base + ROW_SUB, :] = _dot(h2_ref[rows, :], wg_ref[...])
        val = _dot(h2_ref[rows, :], wv_ref[...])
        conv = cb_ref[...]
        for k in range(n_taps):
            conv = conv + (ext_ref[base - k:base - k + ROW_SUB, :]
                           * cw_ref[n_taps - 1 - k:n_taps - k, :])
        act = (jax.nn.gelu(conv) * val).astype(BF16)
        acc_ref[rows, :] += _dot(act, wd_ref[...])

    carry_ref[f] = ext_ref[t_rows:t_rows + CONV_HALO, :]

    @pl.when(f == pl.num_programs(2) - 1)
    def _():
        y = x_ref[...] + acc_ref[...]
        if apply_final:
            y = _rms_norm(y, gfin_ref[...])
        o_ref[...] = y


def _ffn(x, g, w_up, conv_w, conv_b, w_down, g_final, apply_final, batch, seq):
    n, d = x.shape
    d_ff = w_down.shape[0]
    t, tf = FFN_T, FFN_TF
    assert seq % t == 0 and d_ff % tf == 0
    nblk, nf = seq // t, d_ff // tf
    row = lambda b, i, f: (b * nblk + i, 0)
    return pl.pallas_call(
        functools.partial(_ffn_kernel, apply_final),
        grid=(batch, nblk, nf),
        in_specs=[
            pl.BlockSpec((t, d), row),
            pl.BlockSpec((1, d), lambda b, i, f: (0, 0)),
            pl.BlockSpec((d, tf), lambda b, i, f: (0, f)),
            pl.BlockSpec((d, tf), lambda b, i, f: (0, nf + f)),
            pl.BlockSpec((conv_w.shape[0], tf), lambda b, i, f: (0, f)),
            pl.BlockSpec((1, tf), lambda b, i, f: (0, f)),
            pl.BlockSpec((tf, d), lambda b, i, f: (f, 0)),
            pl.BlockSpec((1, d), lambda b, i, f: (0, 0)),
        ],
        out_specs=pl.BlockSpec((t, d), row),
        out_shape=jax.ShapeDtypeStruct((n, d), F32),
        scratch_shapes=[
            pltpu.VMEM((t, d), BF16),
            pltpu.VMEM((t, d), F32),
            pltpu.VMEM((CONV_HALO + t, tf), F32),
            pltpu.VMEM((nf, CONV_HALO, tf), F32),
        ],
        compiler_params=pltpu.CompilerParams(
            dimension_semantics=("arbitrary", "arbitrary", "arbitrary"),
            vmem_limit_bytes=VMEM_LIMIT_BYTES),
    )(x, g, w_up, w_up, conv_w, conv_b, w_down, g_final)


def kernel(x, g_mix, w_in, b_gate, w_pool, pool_scale, lru_conv_w, lru_conv_b, w_a, b_a, w_i,
           b_i, lru_lambda, w_pool_proj, w_lru_proj, w_out, g_mlp, w_up, ffn_conv_w, ffn_conv_b,
           w_down, g_final):
    batch, seq, d = x.shape
    depth = w_in.shape[0]
    pool_width = pool_scale.shape[1]
    lru_width = lru_lambda.shape[1]
    mix_width = pool_width + lru_width
    row2 = lambda v: v.reshape(1, -1)
    xf = x.reshape(batch * seq, d)
    for l in range(depth):
        w_in_bf = w_in[l].astype(BF16)
        d_pool, v, h, wdn_bf = _mixproj(xf, row2(g_mix[l]), w_in_bf, lru_conv_w[l],
                                        row2(lru_conv_b[l]), w_down[l], pool_width, mix_width, seq)
        act_gelu, gates, wup_bf = _gateproj(h, w_in_bf, row2(b_gate[l]), w_up[l], mix_width,
                                            lru_width)
        y_pool, y_lru, wpp_bf, wlp_bf, wout_bf = _mixer(
            d_pool, v, act_gelu, w_pool[l].astype(BF16), row2(pool_scale[l]),
            w_a[l].astype(BF16), row2(b_a[l]), w_i[l].astype(BF16), row2(b_i[l]),
            row2(lru_lambda[l]), (w_pool_proj[l], w_lru_proj[l], w_out[l]), batch, seq)
        xf = _merge(y_pool, y_lru, gates, xf, wpp_bf, wlp_bf, wout_bf)
        xf = _ffn(xf, row2(g_mlp[l]), wup_bf, ffn_conv_w[l], row2(ffn_conv_b[l]),
                  wdn_bf, row2(g_final), l == depth - 1, batch, seq)
    return xf.reshape(batch, seq, d)
```

```python
import functools

import jax
import jax.numpy as jnp
from jax import lax
from jax.experimental import pallas as pl
from jax.experimental.pallas import tpu as pltpu

F32 = jnp.float32
BF16 = jnp.bfloat16

POOL_WINDOWS = (2, 4, 8, 16)
POOL_GROUP_WIDTH = 256
LRU_BLOCK_WIDTH = 256
LRU_C = 8.0
EPS = 1e-6

SUBLANES = 8
BF16_SUBLANES = 16
POOL_SUB = 4
POOL_SUB_HALO = 16
SEQ_HALO = 24
CONV_HALO = SUBLANES
VMEM_LIMIT_BYTES = 56 * 1024 * 1024

CAST_ROWS = 256
ROW_SUB = 256
PROJ_TN = 1024
MIXPROJ_TM = 512
GATEPROJ_TM = 512
GATEPROJ_SUB = 256
MIXER_T = 512
MERGE_TM = 512
MERGE_CHUNK = 512
FFN_T = 512
FFN_TF = 1024


def _rms_norm(x, g):
    ms = jnp.mean(x * x, axis=-1, keepdims=True)
    return x * lax.rsqrt(ms + EPS) * g


def _dot(a, b):
    return jnp.dot(a, b, preferred_element_type=F32)


def _sigmoid(x):
    return 0.5 * jnp.tanh(0.5 * x) + 0.5


def _resident(shape):
    zeros = (0,) * len(shape)
    return pl.BlockSpec(shape, lambda *_: zeros, pipeline_mode=pl.Buffered(1))


def _slab_specs(weights, n_steps, step_index):
    specs = []
    for w in weights:
        rows, cols = w.shape
        assert rows % (n_steps * BF16_SUBLANES) == 0
        specs.append(pl.BlockSpec((rows // n_steps, cols), step_index))
    return specs


def _cast_slabs(src_refs, dst_refs):
    for src, dst in zip(src_refs, dst_refs):
        dst[...] = src[...].astype(BF16)


def _cast_kernel(src_ref, dst_ref):
    dst_ref[...] = src_ref[...].astype(BF16)


def _cast_first_columns(w, width):
    rows = w.shape[0]
    assert rows % CAST_ROWS == 0 and w.shape[1] % width == 0
    return pl.pallas_call(
        _cast_kernel,
        grid=(rows // CAST_ROWS,),
        in_specs=[pl.BlockSpec((CAST_ROWS, width), lambda i: (i, 0))],
        out_specs=pl.BlockSpec((CAST_ROWS, width), lambda i: (i, 0)),
        out_shape=jax.ShapeDtypeStruct((rows, width), BF16),
    )(w)


def _mixproj_kernel(blocks_per_seq, n_cast, x_ref, g_ref, w_ref, convw_ref, convb_ref, *refs):
    cast_refs = refs[:n_cast]
    d_ref, v_ref, hout_ref = refs[n_cast:n_cast + 3]
    cast_bf_refs = refs[n_cast + 3:2 * n_cast + 3]
    h_ref, ext_ref, sub_ref, carry_ref = refs[2 * n_cast + 3:]
    _cast_slabs(cast_refs, cast_bf_refs)
    tn = PROJ_TN
    n_sub = x_ref.shape[0] // ROW_SUB
    block_in_seq = lax.rem(pl.program_id(0), blocks_per_seq)

    @pl.when(block_in_seq == 0)
    def _():
        carry_ref[...] = jnp.zeros_like(carry_ref)

    @pl.loop(0, n_sub)
    def _(r):
        rows = pl.ds(pl.multiple_of(r * ROW_SUB, ROW_SUB), ROW_SUB)
        h = _rms_norm(x_ref[rows, :], g_ref[...]).astype(BF16)
        h_ref[rows, :] = h
        hout_ref[rows, :] = h

    def project(tile, r):
        slot = r % 2
        if r == 0:
            ext_ref[slot, 0:SEQ_HALO, :] = carry_ref[tile]
        ext_ref[slot, SEQ_HALO:SEQ_HALO + ROW_SUB, :] = _dot(
            h_ref[r * ROW_SUB:(r + 1) * ROW_SUB, :], w_ref[:, tile * tn:(tile + 1) * tn])
        return slot

    def pass_tail(tile, r):
        tail = ext_ref[r % 2, ROW_SUB:ROW_SUB + SEQ_HALO, :]
        if r + 1 < n_sub:
            ext_ref[(r + 1) % 2, 0:SEQ_HALO, :] = tail
        else:
            carry_ref[tile] = tail

    def window(load, q0, n, count, stride):
        s = load(q0, n)
        for m in range(1, count):
            s = s + load(q0 - m * stride, n)
        return s

    wide = [g for g, w in enumerate(POOL_WINDOWS) if w > POOL_SUB]
    for r in range(n_sub):
        rows = slice(r * ROW_SUB, (r + 1) * ROW_SUB)
        slot = project(0, r)
        pos = (block_in_seq * x_ref.shape[0] + r * ROW_SUB + 1
               + lax.broadcasted_iota(jnp.int32, (ROW_SUB, 1), 0)).astype(F32)
        for g, w in enumerate(POOL_WINDOWS):
            cols = slice(g * POOL_GROUP_WIDTH, (g + 1) * POOL_GROUP_WIDTH)
            load_ext = lambda q0, n: ext_ref[slot, q0:q0 + n, cols]
            if w <= POOL_SUB:
                s = window(load_ext, SEQ_HALO, ROW_SUB, w, 1)
            else:
                scols = slice(wide.index(g) * POOL_GROUP_WIDTH,
                              (wide.index(g) + 1) * POOL_GROUP_WIDTH)
                n = POOL_SUB_HALO + ROW_SUB
                sub_ref[slot, 0:n, scols] = window(
                    load_ext, SEQ_HALO - POOL_SUB_HALO, n, POOL_SUB, 1)
                load_sub = lambda q0, n: sub_ref[slot, q0:q0 + n, scols]
                s = window(load_sub, POOL_SUB_HALO, ROW_SUB, w // POOL_SUB, POOL_SUB)
            inv_count = 1.0 / jnp.minimum(pos, float(w))
            d = s * inv_count - ext_ref[slot, SEQ_HALO:SEQ_HALO + ROW_SUB, cols]
            d_ref[rows, cols] = d.astype(BF16)
        pass_tail(0, r)

    n_taps = convw_ref.shape[0]
    for tile in range(1, w_ref.shape[1] // tn):
        cols = slice((tile - 1) * tn, tile * tn)
        for r in range(n_sub):
            slot = project(tile, r)
            v = convb_ref[:, cols]
            for k in range(n_taps):
                v = v + (ext_ref[slot, SEQ_HALO - k:SEQ_HALO - k + ROW_SUB, :]
                         * convw_ref[n_taps - 1 - k:n_taps - k, cols])
            v_ref[r * ROW_SUB:(r + 1) * ROW_SUB, cols] = v
            pass_tail(tile, r)


def _mixproj(x, g, w_mix, w_in, conv_w, conv_b, cast_weight, pool_width, seq):
    n, d = x.shape
    mix_width = w_mix.shape[1]
    lru_width = mix_width - pool_width
    tm, tn = MIXPROJ_TM, PROJ_TN
    assert pool_width == tn == len(POOL_WINDOWS) * POOL_GROUP_WIDTH and lru_width % tn == 0
    assert n % tm == 0 and seq % tm == 0 and tm % ROW_SUB == 0
    n_wide = sum(w > POOL_SUB for w in POOL_WINDOWS)
    n_steps = n // tm
    row = lambda i: (i, 0)
    assert w_in.shape[1] % mix_width == 0 and d % (n_steps * BF16_SUBLANES) == 0
    n_gblocks = w_in.shape[1] // mix_width - 1
    gate_in_specs = [pl.BlockSpec((d // n_steps, mix_width),
                                  functools.partial(lambda k, i: (i, k), 1 + k))
                     for k in range(n_gblocks)]
    gate_out_specs = [pl.BlockSpec((d // n_steps, mix_width), row)] * n_gblocks
    return pl.pallas_call(
        functools.partial(_mixproj_kernel, seq // tm, n_gblocks + 1),
        grid=(n_steps,),
        in_specs=[
            pl.BlockSpec((tm, d), row),
            _resident((1, d)), _resident(w_mix.shape),
            _resident(conv_w.shape), _resident(conv_b.shape),
            *gate_in_specs, *_slab_specs((cast_weight,), n_steps, row),
        ],
        out_specs=[pl.BlockSpec((tm, pool_width), row), pl.BlockSpec((tm, lru_width), row),
                   pl.BlockSpec((tm, d), row),
                   *gate_out_specs, *_slab_specs((cast_weight,), n_steps, row)],
        out_shape=[
            jax.ShapeDtypeStruct((n, pool_width), BF16),
            jax.ShapeDtypeStruct((n, lru_width), F32),
            jax.ShapeDtypeStruct((n, d), BF16),
            *[jax.ShapeDtypeStruct((d, mix_width), BF16)] * n_gblocks,
            jax.ShapeDtypeStruct(cast_weight.shape, BF16),
        ],
        scratch_shapes=[
            pltpu.VMEM((tm, d), BF16),
            pltpu.VMEM((2, SEQ_HALO + ROW_SUB, tn), F32),
            pltpu.VMEM((2, POOL_SUB_HALO + ROW_SUB, n_wide * POOL_GROUP_WIDTH), F32),
            pltpu.VMEM((mix_width // tn, SEQ_HALO, tn), F32),
        ],
        compiler_params=pltpu.CompilerParams(
            dimension_semantics=("arbitrary",),
            vmem_limit_bytes=VMEM_LIMIT_BYTES),
    )(x, g, w_mix, conv_w, conv_b, *([w_in] * n_gblocks), cast_weight)


def _gateproj_kernel(n_gelu, n_wblocks, h_ref, *refs):
    w_refs = refs[:n_wblocks]
    b_ref, cast_ref, gelu_ref, gates_ref, cast_bf_ref = refs[n_wblocks:]
    _cast_slabs((cast_ref,), (cast_bf_ref,))
    tn = PROJ_TN
    tiles_per_block = w_refs[0].shape[1] // tn
    for tile in range(n_wblocks * tiles_per_block):
        w_ref = w_refs[tile // tiles_per_block]
        wcols = slice((tile % tiles_per_block) * tn, (tile % tiles_per_block + 1) * tn)
        for r in range(h_ref.shape[0] // GATEPROJ_SUB):
            rows = slice(r * GATEPROJ_SUB, (r + 1) * GATEPROJ_SUB)
            p = _dot(h_ref[rows, :], w_ref[:, wcols])
            if tile < n_gelu:
                gelu_ref[rows, tile * tn:(tile + 1) * tn] = jax.nn.gelu(p).astype(BF16)
            else:
                cols = slice((tile - n_gelu) * tn, (tile - n_gelu + 1) * tn)
                gates_ref[rows, cols] = jax.nn.sigmoid(p + b_ref[:, cols]).astype(BF16)


def _gateproj(h, w_blocks, b_gate, cast_weight, gelu_width):
    n, d = h.shape
    n_wblocks = len(w_blocks)
    gate_width = sum(w.shape[1] for w in w_blocks) - gelu_width
    tm, tn = GATEPROJ_TM, PROJ_TN
    assert n % tm == 0 and gelu_width % tn == 0 and gate_width % tn == 0
    assert all(w.shape == w_blocks[0].shape for w in w_blocks) and w_blocks[0].shape[1] % tn == 0
    row = lambda i: (i, 0)
    w_specs = [_resident(w.shape) for w in w_blocks]
    return pl.pallas_call(
        functools.partial(_gateproj_kernel, gelu_width // tn, n_wblocks),
        grid=(n // tm,),
        in_specs=[pl.BlockSpec((tm, d), row), *w_specs, _resident(b_gate.shape),
                  *_slab_specs((cast_weight,), n // tm, row)],
        out_specs=[pl.BlockSpec((tm, gelu_width), row), pl.BlockSpec((tm, gate_width), row),
                   *_slab_specs((cast_weight,), n // tm, row)],
        out_shape=[jax.ShapeDtypeStruct((n, gelu_width), BF16),
                   jax.ShapeDtypeStruct((n, gate_width), BF16),
                   jax.ShapeDtypeStruct(cast_weight.shape, BF16)],
        compiler_params=pltpu.CompilerParams(
            dimension_semantics=("arbitrary",),
            vmem_limit_bytes=VMEM_LIMIT_BYTES),
    )(h, *w_blocks, b_gate, cast_weight)


def _mixer_kernel(d_ref, v_ref, gelu_ref, wpool_ref, pscale_ref,
                  wa_ref, ba_ref, wi_ref, bi_ref, lam_ref, wpp_ref, wlp_ref, wout_ref,
                  ypool_ref, ylru_ref, wpp_bf_ref, wlp_bf_ref, wout_bf_ref,
                  a_ref, bx_ref, h_ref, hcarry_ref):
    _cast_slabs((wpp_ref, wlp_ref, wout_ref), (wpp_bf_ref, wlp_bf_ref, wout_bf_ref))

    t_rows = ylru_ref.shape[0]
    lru_width = ylru_ref.shape[1]

    @pl.when(pl.program_id(1) == 0)
    def _():
        hcarry_ref[...] = jnp.zeros_like(hcarry_ref)

    for g in range(len(POOL_WINDOWS)):
        cols = slice(g * POOL_GROUP_WIDTH, (g + 1) * POOL_GROUP_WIDTH)
        y = _dot(d_ref[:, cols], wpool_ref[g]) * pscale_ref[:, cols]
        ypool_ref[:, cols] = y.astype(BF16)

    softplus_neg_lam = jax.nn.softplus(-lam_ref[...])
    for hd in range(lru_width // LRU_BLOCK_WIDTH):
        cols = slice(hd * LRU_BLOCK_WIDTH, (hd + 1) * LRU_BLOCK_WIDTH)
        v = v_ref[:, cols]
        vb = v.astype(BF16)
        r = _sigmoid(_dot(vb, wa_ref[hd]) + ba_ref[:, cols])
        ig = _sigmoid(_dot(vb, wi_ref[hd]) + bi_ref[:, cols])
        log_a = (-LRU_C) * r * softplus_neg_lam[:, cols]
        a = jnp.exp(log_a)
        q = -jnp.tanh(log_a) * (a * a + 1.0)
        mult = jnp.where(q > 0.0, q * lax.rsqrt(q), 0.0)
        a_ref[:, cols] = a
        bx_ref[:, cols] = mult * (ig * v)

    def step(t, h):
        h = a_ref[pl.ds(t, 1), :] * h + bx_ref[pl.ds(t, 1), :]
        h_ref[pl.ds(t, 1), :] = h
        return h

    hcarry_ref[...] = lax.fori_loop(0, t_rows, step, hcarry_ref[...], unroll=8)
    ylru_ref[...] = (h_ref[...] * gelu_ref[...].astype(F32)).astype(BF16)


def _mixer(d_pool, v, act_gelu, w_pool, pool_scale, w_a, b_a, w_i, b_i, lam,
           cast_weights, batch, seq):
    n, pool_width = d_pool.shape
    lru_width = v.shape[1]
    t = MIXER_T
    assert seq % t == 0
    nblk = seq // t
    row = lambda b, i: (b * nblk + i, 0)
    return pl.pallas_call(
        _mixer_kernel,
        grid=(batch, nblk),
        in_specs=[
            pl.BlockSpec((t, pool_width), row),
            pl.BlockSpec((t, lru_width), row),
            pl.BlockSpec((t, lru_width), row),
            _resident(w_pool.shape), _resident(pool_scale.shape),
            _resident(w_a.shape), _resident(b_a.shape),
            _resident(w_i.shape), _resident(b_i.shape), _resident(lam.shape),
            *_slab_specs(cast_weights, batch * nblk, row),
        ],
        out_specs=[pl.BlockSpec((t, pool_width), row), pl.BlockSpec((t, lru_width), row),
                   *_slab_specs(cast_weights, batch * nblk, row)],
        out_shape=[jax.ShapeDtypeStruct((n, pool_width), BF16),
                   jax.ShapeDtypeStruct((n, lru_width), BF16),
                   *[jax.ShapeDtypeStruct(w.shape, BF16) for w in cast_weights]],
        scratch_shapes=[
            pltpu.VMEM((t, lru_width), F32),
            pltpu.VMEM((t, lru_width), F32),
            pltpu.VMEM((t, lru_width), F32),
            pltpu.VMEM((1, lru_width), F32),
        ],
        compiler_params=pltpu.CompilerParams(
            dimension_semantics=("arbitrary", "arbitrary"),
            vmem_limit_bytes=VMEM_LIMIT_BYTES),
    )(d_pool, v, act_gelu, w_pool, pool_scale, w_a, b_a, w_i, b_i, lam, *cast_weights)


def _merge_kernel(yp_ref, yl_ref, gates_ref, x_ref, wpp_ref, wlp_ref, wout_ref, o_ref, m_ref):
    d = o_ref.shape[1]
    ch = MERGE_CHUNK
    for c in range(d // ch):
        cols = slice(c * ch, (c + 1) * ch)
        p = _dot(yp_ref[...], wpp_ref[:, cols])
        q = _dot(yl_ref[...], wlp_ref[:, cols])
        g_pool = gates_ref[:, cols].astype(F32)
        g_lru = gates_ref[:, d + c * ch:d + (c + 1) * ch].astype(F32)
        m_ref[:, cols] = (g_pool * p + g_lru * q).astype(BF16)
    for c in range(d // ch):
        cols = slice(c * ch, (c + 1) * ch)
        o_ref[:, cols] = x_ref[:, cols] + _dot(m_ref[...], wout_ref[:, cols])


def _merge(y_pool, y_lru, gates, x, w_pool_proj, w_lru_proj, w_out):
    n, d = x.shape
    tm = MERGE_TM
    assert n % tm == 0 and d % MERGE_CHUNK == 0
    row = lambda i: (i, 0)
    return pl.pallas_call(
        _merge_kernel,
        grid=(n // tm,),
        in_specs=[
            pl.BlockSpec((tm, y_pool.shape[1]), row),
            pl.BlockSpec((tm, y_lru.shape[1]), row),
            pl.BlockSpec((tm, gates.shape[1]), row),
            pl.BlockSpec((tm, d), row),
            _resident(w_pool_proj.shape), _resident(w_lru_proj.shape), _resident(w_out.shape),
        ],
        out_specs=pl.BlockSpec((tm, d), row),
        out_shape=jax.ShapeDtypeStruct((n, d), F32),
        scratch_shapes=[pltpu.VMEM((tm, d), BF16)],
        compiler_params=pltpu.CompilerParams(
            dimension_semantics=("arbitrary",),
            vmem_limit_bytes=VMEM_LIMIT_BYTES),
    )(y_pool, y_lru, gates, x, w_pool_proj, w_lru_proj, w_out)


def _ffn_kernel(apply_final, x_ref, g_ref, wg_ref, wv_ref, cw_ref, cb_ref, wd_ref, gfin_ref,
                o_ref, h2_ref, acc_ref, ext_ref, carry_ref):
    i = pl.program_id(1)
    f = pl.program_id(2)
    t_rows = x_ref.shape[0]

    @pl.when(f == 0)
    def _():
        h2_ref[...] = _rms_norm(x_ref[...], g_ref[...]).astype(BF16)
        acc_ref[...] = jnp.zeros_like(acc_ref)

    @pl.when(i == 0)
    def _():
        ext_ref[0:CONV_HALO, :] = jnp.zeros((CONV_HALO, ext_ref.shape[1]), F32)

    @pl.when(i > 0)
    def _():
        ext_ref[0:CONV_HALO, :] = carry_ref[f]

    n_taps = cw_ref.shape[0]
    for r in range(t_rows // ROW_SUB):
        rows = slice(r * ROW_SUB, (r + 1) * ROW_SUB)
        base = CONV_HALO + r * ROW_SUB
        ext_ref[base:base + ROW_SUB, :] = _dot(h2_ref[rows, :], wg_ref[...])
        val = _dot(h2_ref[rows, :], wv_ref[...])
        conv = cb_ref[...]
        for k in range(n_taps):
            conv = conv + (ext_ref[base - k:base - k + ROW_SUB, :]
                           * cw_ref[n_taps - 1 - k:n_taps - k, :])
        act = (jax.nn.gelu(conv) * val).astype(BF16)
        acc_ref[rows, :] += _dot(act, wd_ref[...])

    carry_ref[f] = ext_ref[t_rows:t_rows + CONV_HALO, :]

    @pl.when(f == pl.num_programs(2) - 1)
    def _():
        y = x_ref[...] + acc_ref[...]
        if apply_final:
            y = _rms_norm(y, gfin_ref[...])
        o_ref[...] = y


def _ffn(x, g, w_up, conv_w, conv_b, w_down, g_final, apply_final, batch, seq):
    n, d = x.shape
    d_ff = w_down.shape[0]
    t, tf = FFN_T, FFN_TF
    assert seq % t == 0 and d_ff % tf == 0
    nblk, nf = seq // t, d_ff // tf
    row = lambda b, i, f: (b * nblk + i, 0)
    return pl.pallas_call(
        functools.partial(_ffn_kernel, apply_final),
        grid=(batch, nblk, nf),
        in_specs=[
            pl.BlockSpec((t, d), row),
            pl.BlockSpec((1, d), lambda b, i, f: (0, 0)),
            pl.BlockSpec((d, tf), lambda b, i, f: (0, f)),
            pl.BlockSpec((d, tf), lambda b, i, f: (0, nf + f)),
            pl.BlockSpec((conv_w.shape[0], tf), lambda b, i, f: (0, f)),
            pl.BlockSpec((1, tf), lambda b, i, f: (0, f)),
            pl.BlockSpec((tf, d), lambda b, i, f: (f, 0)),
            pl.BlockSpec((1, d), lambda b, i, f: (0, 0)),
        ],
        out_specs=pl.BlockSpec((t, d), row),
        out_shape=jax.ShapeDtypeStruct((n, d), F32),
        scratch_shapes=[
            pltpu.VMEM((t, d), BF16),
            pltpu.VMEM((t, d), F32),
            pltpu.VMEM((CONV_HALO + t, tf), F32),
            pltpu.VMEM((nf, CONV_HALO, tf), F32),
        ],
        compiler_params=pltpu.CompilerParams(
            dimension_semantics=("arbitrary", "arbitrary", "arbitrary"),
            vmem_limit_bytes=VMEM_LIMIT_BYTES),
    )(x, g, w_up, w_up, conv_w, conv_b, w_down, g_final)


def kernel(x, g_mix, w_in, b_gate, w_pool, pool_scale, lru_conv_w, lru_conv_b, w_a, b_a, w_i,
           b_i, lru_lambda, w_pool_proj, w_lru_proj, w_out, g_mlp, w_up, ffn_conv_w, ffn_conv_b,
           w_down, g_final):
    batch, seq, d = x.shape
    depth = w_in.shape[0]
    pool_width = pool_scale.shape[1]
    lru_width = lru_lambda.shape[1]
    mix_width = pool_width + lru_width
    row2 = lambda v: v.reshape(1, -1)
    xf = x.reshape(batch * seq, d)
    for l in range(depth):
        d_pool, v, h, *w_gate_bf, wdn_bf = _mixproj(
            xf, row2(g_mix[l]), _cast_first_columns(w_in[l], mix_width), w_in[l], lru_conv_w[l],
            row2(lru_conv_b[l]), w_down[l], pool_width, seq)
        act_gelu, gates, wup_bf = _gateproj(h, w_gate_bf, row2(b_gate[l]), w_up[l], lru_width)
        y_pool, y_lru, wpp_bf, wlp_bf, wout_bf = _mixer(
            d_pool, v, act_gelu, w_pool[l].astype(BF16), row2(pool_scale[l]),
            w_a[l].astype(BF16), row2(b_a[l]), w_i[l].astype(BF16), row2(b_i[l]),
            row2(lru_lambda[l]), (w_pool_proj[l], w_lru_proj[l], w_out[l]), batch, seq)
        xf = _merge(y_pool, y_lru, gates, xf, wpp_bf, wlp_bf, wout_bf)
        xf = _ffn(xf, row2(g_mlp[l]), wup_bf, ffn_conv_w[l], row2(ffn_conv_b[l]),
                  wdn_bf, row2(g_final), l == depth - 1, batch, seq)
    return xf.reshape(batch, seq, d)
```

```python
import functools

import jax
import jax.numpy as jnp
from jax import lax
from jax.experimental import pallas as pl
from jax.experimental.pallas import tpu as pltpu

F32 = jnp.float32
BF16 = jnp.bfloat16

POOL_WINDOWS = (2, 4, 8, 16)
POOL_GROUP_WIDTH = 256
LRU_BLOCK_WIDTH = 256
LRU_C = 8.0
EPS = 1e-6

SUBLANES = 8
BF16_SUBLANES = 16
POOL_SUB = 4
POOL_SUB_HALO = 16
SEQ_HALO = 24
CONV_HALO = SUBLANES
VMEM_LIMIT_BYTES = 56 * 1024 * 1024

CAST_ROWS = 256
ROW_SUB = 256
PROJ_TN = 1024
MIXPROJ_TM = 512
GATEPROJ_TM = 512
GATEPROJ_SUB = 256
GATEPROJ_GROUP = 2
MIXER_T = 512
MERGE_TM = 512
MERGE_CHUNK = 512
FFN_T = 512
FFN_TF = 1024


def _rms_norm(x, g):
    ms = jnp.mean(x * x, axis=-1, keepdims=True)
    return x * lax.rsqrt(ms + EPS) * g


def _dot(a, b):
    return jnp.dot(a, b, preferred_element_type=F32)


def _sigmoid(x):
    return 0.5 * jnp.tanh(0.5 * x) + 0.5


def _resident(shape):
    zeros = (0,) * len(shape)
    return pl.BlockSpec(shape, lambda *_: zeros, pipeline_mode=pl.Buffered(1))


def _slab_specs(weights, n_steps, step_index):
    specs = []
    for w in weights:
        rows, cols = w.shape
        assert rows % (n_steps * BF16_SUBLANES) == 0
        specs.append(pl.BlockSpec((rows // n_steps, cols), step_index))
    return specs


def _cast_slabs(src_refs, dst_refs):
    for src, dst in zip(src_refs, dst_refs):
        dst[...] = src[...].astype(BF16)


def _cast_kernel(src_ref, dst_ref):
    dst_ref[...] = src_ref[...].astype(BF16)


def _cast_first_columns(w, width):
    rows = w.shape[0]
    assert rows % CAST_ROWS == 0 and w.shape[1] % width == 0
    return pl.pallas_call(
        _cast_kernel,
        grid=(rows // CAST_ROWS,),
        in_specs=[pl.BlockSpec((CAST_ROWS, width), lambda i: (i, 0))],
        out_specs=pl.BlockSpec((CAST_ROWS, width), lambda i: (i, 0)),
        out_shape=jax.ShapeDtypeStruct((rows, width), BF16),
    )(w)


def _mixproj_kernel(blocks_per_seq, n_cast, x_ref, g_ref, w_ref, convw_ref, convb_ref, *refs):
    cast_refs = refs[:n_cast]
    d_ref, v_ref, hout_ref = refs[n_cast:n_cast + 3]
    cast_bf_refs = refs[n_cast + 3:2 * n_cast + 3]
    h_ref, ext_ref, sub_ref, carry_ref = refs[2 * n_cast + 3:]
    _cast_slabs(cast_refs, cast_bf_refs)
    tn = PROJ_TN
    n_sub = x_ref.shape[0] // ROW_SUB
    block_in_seq = lax.rem(pl.program_id(0), blocks_per_seq)

    @pl.when(block_in_seq == 0)
    def _():
        carry_ref[...] = jnp.zeros_like(carry_ref)

    @pl.loop(0, n_sub)
    def _(r):
        rows = pl.ds(pl.multiple_of(r * ROW_SUB, ROW_SUB), ROW_SUB)
        h = _rms_norm(x_ref[rows, :], g_ref[...]).astype(BF16)
        h_ref[rows, :] = h
        hout_ref[rows, :] = h

    def project(tile, r):
        slot = r % 2
        if r == 0:
            ext_ref[slot, 0:SEQ_HALO, :] = carry_ref[tile]
        ext_ref[slot, SEQ_HALO:SEQ_HALO + ROW_SUB, :] = _dot(
            h_ref[r * ROW_SUB:(r + 1) * ROW_SUB, :], w_ref[:, tile * tn:(tile + 1) * tn])
        return slot

    def pass_tail(tile, r):
        tail = ext_ref[r % 2, ROW_SUB:ROW_SUB + SEQ_HALO, :]
        if r + 1 < n_sub:
            ext_ref[(r + 1) % 2, 0:SEQ_HALO, :] = tail
        else:
            carry_ref[tile] = tail

    def window(load, q0, n, count, stride):
        s = load(q0, n)
        for m in range(1, count):
            s = s + load(q0 - m * stride, n)
        return s

    wide = [g for g, w in enumerate(POOL_WINDOWS) if w > POOL_SUB]
    for r in range(n_sub):
        rows = slice(r * ROW_SUB, (r + 1) * ROW_SUB)
        slot = project(0, r)
        pos = (block_in_seq * x_ref.shape[0] + r * ROW_SUB + 1
               + lax.broadcasted_iota(jnp.int32, (ROW_SUB, 1), 0)).astype(F32)
        for g, w in enumerate(POOL_WINDOWS):
            cols = slice(g * POOL_GROUP_WIDTH, (g + 1) * POOL_GROUP_WIDTH)
            load_ext = lambda q0, n: ext_ref[slot, q0:q0 + n, cols]
            if w <= POOL_SUB:
                s = window(load_ext, SEQ_HALO, ROW_SUB, w, 1)
            else:
                scols = slice(wide.index(g) * POOL_GROUP_WIDTH,
                              (wide.index(g) + 1) * POOL_GROUP_WIDTH)
                n = POOL_SUB_HALO + ROW_SUB
                sub_ref[slot, 0:n, scols] = window(
                    load_ext, SEQ_HALO - POOL_SUB_HALO, n, POOL_SUB, 1)
                load_sub = lambda q0, n: sub_ref[slot, q0:q0 + n, scols]
                s = window(load_sub, POOL_SUB_HALO, ROW_SUB, w // POOL_SUB, POOL_SUB)
            inv_count = 1.0 / jnp.minimum(pos, float(w))
            d = s * inv_count - ext_ref[slot, SEQ_HALO:SEQ_HALO + ROW_SUB, cols]
            d_ref[rows, cols] = d.astype(BF16)
        pass_tail(0, r)

    n_taps = convw_ref.shape[0]
    for tile in range(1, w_ref.shape[1] // tn):
        cols = slice((tile - 1) * tn, tile * tn)
        for r in range(n_sub):
            slot = project(tile, r)
            v = convb_ref[:, cols]
            for k in range(n_taps):
                v = v + (ext_ref[slot, SEQ_HALO - k:SEQ_HALO - k + ROW_SUB, :]
                         * convw_ref[n_taps - 1 - k:n_taps - k, cols])
            v_ref[r * ROW_SUB:(r + 1) * ROW_SUB, cols] = v
            pass_tail(tile, r)


def _mixproj(x, g, w_mix, w_in, conv_w, conv_b, cast_weight, pool_width, seq):
    n, d = x.shape
    mix_width = w_mix.shape[1]
    lru_width = mix_width - pool_width
    tm, tn = MIXPROJ_TM, PROJ_TN
    assert pool_width == tn == len(POOL_WINDOWS) * POOL_GROUP_WIDTH and lru_width % tn == 0
    assert n % tm == 0 and seq % tm == 0 and tm % ROW_SUB == 0
    n_wide = sum(w > POOL_SUB for w in POOL_WINDOWS)
    n_steps = n // tm
    row = lambda i: (i, 0)
    assert w_in.shape[1] % mix_width == 0 and d % (n_steps * BF16_SUBLANES) == 0
    n_gblocks = w_in.shape[1] // mix_width - 1
    gate_in_specs = [pl.BlockSpec((d // n_steps, mix_width),
                                  functools.partial(lambda k, i: (i, k), 1 + k))
                     for k in range(n_gblocks)]
    gate_out_specs = [pl.BlockSpec((d // n_steps, mix_width), row)] * n_gblocks
    return pl.pallas_call(
        functools.partial(_mixproj_kernel, seq // tm, n_gblocks + 1),
        grid=(n_steps,),
        in_specs=[
            pl.BlockSpec((tm, d), row),
            _resident((1, d)), _resident(w_mix.shape),
            _resident(conv_w.shape), _resident(conv_b.shape),
            *gate_in_specs, *_slab_specs((cast_weight,), n_steps, row),
        ],
        out_specs=[pl.BlockSpec((tm, pool_width), row), pl.BlockSpec((tm, lru_width), row),
                   pl.BlockSpec((tm, d), row),
                   *gate_out_specs, *_slab_specs((cast_weight,), n_steps, row)],
        out_shape=[
            jax.ShapeDtypeStruct((n, pool_width), BF16),
            jax.ShapeDtypeStruct((n, lru_width), F32),
            jax.ShapeDtypeStruct((n, d), BF16),
            *[jax.ShapeDtypeStruct((d, mix_width), BF16)] * n_gblocks,
            jax.ShapeDtypeStruct(cast_weight.shape, BF16),
        ],
        scratch_shapes=[
            pltpu.VMEM((tm, d), BF16),
            pltpu.VMEM((2, SEQ_HALO + ROW_SUB, tn), F32),
            pltpu.VMEM((2, POOL_SUB_HALO + ROW_SUB, n_wide * POOL_GROUP_WIDTH), F32),
            pltpu.VMEM((mix_width // tn, SEQ_HALO, tn), F32),
        ],
        compiler_params=pltpu.CompilerParams(
            dimension_semantics=("arbitrary",),
            vmem_limit_bytes=VMEM_LIMIT_BYTES),
    )(x, g, w_mix, conv_w, conv_b, *([w_in] * n_gblocks), cast_weight)


def _gateproj_kernel(n_gelu, n_wblocks, h_ref, *refs):
    w_refs = refs[:n_wblocks]
    b_ref, cast_ref, gelu_ref, gates_ref, cast_bf_ref = refs[n_wblocks:]
    _cast_slabs((cast_ref,), (cast_bf_ref,))
    tn = PROJ_TN
    tiles_per_block = w_refs[0].shape[1] // tn
    n_sub = h_ref.shape[0] // GATEPROJ_SUB
    n_tiles = n_wblocks * tiles_per_block

    def project(tile, r):
        w_ref = w_refs[tile // tiles_per_block]
        wcols = slice((tile % tiles_per_block) * tn, (tile % tiles_per_block + 1) * tn)
        return _dot(h_ref[r * GATEPROJ_SUB:(r + 1) * GATEPROJ_SUB, :], w_ref[:, wcols])

    def finish(tile, r, p):
        rows = slice(r * GATEPROJ_SUB, (r + 1) * GATEPROJ_SUB)
        if tile < n_gelu:
            gelu_ref[rows, tile * tn:(tile + 1) * tn] = jax.nn.gelu(p).astype(BF16)
        else:
            cols = slice((tile - n_gelu) * tn, (tile - n_gelu + 1) * tn)
            gates_ref[rows, cols] = jax.nn.sigmoid(p + b_ref[:, cols]).astype(BF16)

    for t0 in range(0, n_tiles, GATEPROJ_GROUP):
        group = [(tile, r) for tile in range(t0, min(t0 + GATEPROJ_GROUP, n_tiles))
                 for r in range(n_sub)]
        ps = [project(tile, r) for tile, r in group]
        for (tile, r), p in zip(group, ps):
            finish(tile, r, p)


def _gateproj(h, w_blocks, b_gate, cast_weight, gelu_width):
    n, d = h.shape
    n_wblocks = len(w_blocks)
    gate_width = sum(w.shape[1] for w in w_blocks) - gelu_width
    tm, tn = GATEPROJ_TM, PROJ_TN
    assert n % tm == 0 and gelu_width % tn == 0 and gate_width % tn == 0
    assert all(w.shape == w_blocks[0].shape for w in w_blocks) and w_blocks[0].shape[1] % tn == 0
    row = lambda i: (i, 0)
    w_specs = [_resident(w.shape) for w in w_blocks]
    return pl.pallas_call(
        functools.partial(_gateproj_kernel, gelu_width // tn, n_wblocks),
        grid=(n // tm,),
        in_specs=[pl.BlockSpec((tm, d), row), *w_specs, _resident(b_gate.shape),
                  *_slab_specs((cast_weight,), n // tm, row)],
        out_specs=[pl.BlockSpec((tm, gelu_width), row), pl.BlockSpec((tm, gate_width), row),
                   *_slab_specs((cast_weight,), n // tm, row)],
        out_shape=[jax.ShapeDtypeStruct((n, gelu_width), BF16),
                   jax.ShapeDtypeStruct((n, gate_width), BF16),
                   jax.ShapeDtypeStruct(cast_weight.shape, BF16)],
        compiler_params=pltpu.CompilerParams(
            dimension_semantics=("arbitrary",),
            vmem_limit_bytes=VMEM_LIMIT_BYTES),
    )(h, *w_blocks, b_gate, cast_weight)


def _mixer_kernel(d_ref, v_ref, gelu_ref, wpool_ref, pscale_ref,
                  wa_ref, ba_ref, wi_ref, bi_ref, lam_ref, wpp_ref, wlp_ref, wout_ref,
                  ypool_ref, ylru_ref, wpp_bf_ref, wlp_bf_ref, wout_bf_ref,
                  a_ref, bx_ref, h_ref, hcarry_ref):
    _cast_slabs((wpp_ref, wlp_ref, wout_ref), (wpp_bf_ref, wlp_bf_ref, wout_bf_ref))

    t_rows = ylru_ref.shape[0]
    lru_width = ylru_ref.shape[1]

    @pl.when(pl.program_id(1) == 0)
    def _():
        hcarry_ref[...] = jnp.zeros_like(hcarry_ref)

    for g in range(len(POOL_WINDOWS)):
        cols = slice(g * POOL_GROUP_WIDTH, (g + 1) * POOL_GROUP_WIDTH)
        y = _dot(d_ref[:, cols], wpool_ref[g]) * pscale_ref[:, cols]
        ypool_ref[:, cols] = y.astype(BF16)

    half_log_a_scale = (-0.5 * LRU_C) * jax.nn.softplus(-lam_ref[...])
    for hd in range(lru_width // LRU_BLOCK_WIDTH):
        cols = slice(hd * LRU_BLOCK_WIDTH, (hd + 1) * LRU_BLOCK_WIDTH)
        v = v_ref[:, cols]
        vb = v.astype(BF16)
        x_r = _dot(vb, wa_ref[hd]) + ba_ref[:, cols]
        ig = _sigmoid(_dot(vb, wi_ref[hd]) + bi_ref[:, cols])
        log_a = (jnp.tanh(0.5 * x_r) + 1.0) * half_log_a_scale[:, cols]
        a = jnp.exp(log_a)
        q = -jnp.tanh(log_a) * (a * a + 1.0)
        mult = jnp.where(q > 0.0, q * lax.rsqrt(q), 0.0)
        a_ref[:, cols] = a
        bx_ref[:, cols] = mult * (ig * v)

    def step(t, h):
        h = a_ref[pl.ds(t, 1), :] * h + bx_ref[pl.ds(t, 1), :]
        h_ref[pl.ds(t, 1), :] = h
        return h

    hcarry_ref[...] = lax.fori_loop(0, t_rows, step, hcarry_ref[...], unroll=8)
    ylru_ref[...] = (h_ref[...] * gelu_ref[...].astype(F32)).astype(BF16)


def _mixer(d_pool, v, act_gelu, w_pool, pool_scale, w_a, b_a, w_i, b_i, lam,
           cast_weights, batch, seq):
    n, pool_width = d_pool.shape
    lru_width = v.shape[1]
    t = MIXER_T
    assert seq % t == 0
    nblk = seq // t
    row = lambda b, i: (b * nblk + i, 0)
    return pl.pallas_call(
        _mixer_kernel,
        grid=(batch, nblk),
        in_specs=[
            pl.BlockSpec((t, pool_width), row),
            pl.BlockSpec((t, lru_width), row),
            pl.BlockSpec((t, lru_width), row),
            _resident(w_pool.shape), _resident(pool_scale.shape),
            _resident(w_a.shape), _resident(b_a.shape),
            _resident(w_i.shape), _resident(b_i.shape), _resident(lam.shape),
            *_slab_specs(cast_weights, batch * nblk, row),
        ],
        out_specs=[pl.BlockSpec((t, pool_width), row), pl.BlockSpec((t, lru_width), row),
                   *_slab_specs(cast_weights, batch * nblk, row)],
        out_shape=[jax.ShapeDtypeStruct((n, pool_width), BF16),
                   jax.ShapeDtypeStruct((n, lru_width), BF16),
                   *[jax.ShapeDtypeStruct(w.shape, BF16) for w in cast_weights]],
        scratch_shapes=[
            pltpu.VMEM((t, lru_width), F32),
            pltpu.VMEM((t, lru_width), F32),
            pltpu.VMEM((t, lru_width), F32),
            pltpu.VMEM((1, lru_width), F32),
        ],
        compiler_params=pltpu.CompilerParams(
            dimension_semantics=("arbitrary", "arbitrary"),
            vmem_limit_bytes=VMEM_LIMIT_BYTES),
    )(d_pool, v, act_gelu, w_pool, pool_scale, w_a, b_a, w_i, b_i, lam, *cast_weights)


def _merge_kernel(yp_ref, yl_ref, gates_ref, x_ref, wpp_ref, wlp_ref, wout_ref, o_ref, m_ref):
    d = o_ref.shape[1]
    ch = MERGE_CHUNK
    for c in range(d // ch):
        cols = slice(c * ch, (c + 1) * ch)
        p = _dot(yp_ref[...], wpp_ref[:, cols])
        q = _dot(yl_ref[...], wlp_ref[:, cols])
        g_pool = gates_ref[:, cols].astype(F32)
        g_lru = gates_ref[:, d + c * ch:d + (c + 1) * ch].astype(F32)
        m_ref[:, cols] = (g_pool * p + g_lru * q).astype(BF16)
    for c in range(d // ch):
        cols = slice(c * ch, (c + 1) * ch)
        o_ref[:, cols] = x_ref[:, cols] + _dot(m_ref[...], wout_ref[:, cols])


def _merge(y_pool, y_lru, gates, x, w_pool_proj, w_lru_proj, w_out):
    n, d = x.shape
    tm = MERGE_TM
    assert n % tm == 0 and d % MERGE_CHUNK == 0
    row = lambda i: (i, 0)
    return pl.pallas_call(
        _merge_kernel,
        grid=(n // tm,),
        in_specs=[
            pl.BlockSpec((tm, y_pool.shape[1]), row),
            pl.BlockSpec((tm, y_lru.shape[1]), row),
            pl.BlockSpec((tm, gates.shape[1]), row),
            pl.BlockSpec((tm, d), row),
            _resident(w_pool_proj.shape), _resident(w_lru_proj.shape), _resident(w_out.shape),
        ],
        out_specs=pl.BlockSpec((tm, d), row),
        out_shape=jax.ShapeDtypeStruct((n, d), F32),
        scratch_shapes=[pltpu.VMEM((tm, d), BF16)],
        compiler_params=pltpu.CompilerParams(
            dimension_semantics=("arbitrary",),
            vmem_limit_bytes=VMEM_LIMIT_BYTES),
    )(y_pool, y_lru, gates, x, w_pool_proj, w_lru_proj, w_out)


def _ffn_kernel(apply_final, x_ref, g_ref, wg_ref, wv_ref, cw_ref, cb_ref, wd_ref, gfin_ref,
                o_ref, h2_ref, acc_ref, ext_ref, carry_ref):
    i = pl.program_id(1)
    f = pl.program_id(2)
    t_rows = x_ref.shape[0]

    @pl.when(f == 0)
    def _():
        h2_ref[...] = _rms_norm(x_ref[...], g_ref[...]).astype(BF16)
        acc_ref[...] = jnp.zeros_like(acc_ref)

    @pl.when(i == 0)
    def _():
        ext_ref[0:CONV_HALO, :] = jnp.zeros((CONV_HALO, ext_ref.shape[1]), F32)

    @pl.when(i > 0)
    def _():
        ext_ref[0:CONV_HALO, :] = carry_ref[f]

    n_taps = cw_ref.shape[0]
    for r in range(t_rows // ROW_SUB):
        rows = slice(r * ROW_SUB, (r + 1) * ROW_SUB)
        base = CONV_HALO + r * ROW_SUB
        ext_ref[base:base + ROW_SUB, :] = _dot(h2_ref[rows, :], wg_ref[...])
        val = _dot(h2_ref[rows, :], wv_ref[...])
        conv = cb_ref[...]
        for k in range(n_taps):
            conv = conv + (ext_ref[base - k:base - k + ROW_SUB, :]
                           * cw_ref[n_taps - 1 - k:n_taps - k, :])
        act = (jax.nn.gelu(conv) * val).astype(BF16)
        acc_ref[rows, :] += _dot(act, wd_ref[...])

    carry_ref[f] = ext_ref[t_rows:t_rows + CONV_HALO, :]

    @pl.when(f == pl.num_programs(2) - 1)
    def _():
        y = x_ref[...] + acc_ref[...]
        if apply_final:
            y = _rms_norm(y, gfin_ref[...])
        o_ref[...] = y


def _ffn(x, g, w_up, conv_w, conv_b, w_down, g_final, apply_final, batch, seq):
    n, d = x.shape
    d_ff = w_down.shape[0]
    t, tf = FFN_T, FFN_TF
    assert seq % t == 0 and d_ff % tf == 0
    nblk, nf = seq // t, d_ff // tf
    row = lambda b, i, f: (b * nblk + i, 0)
    return pl.pallas_call(
        functools.partial(_ffn_kernel, apply_final),
        grid=(batch, nblk, nf),
        in_specs=[
            pl.BlockSpec((t, d), row),
            pl.BlockSpec((1, d), lambda b, i, f: (0, 0)),
            pl.BlockSpec((d, tf), lambda b, i, f: (0, f)),
            pl.BlockSpec((d, tf), lambda b, i, f: (0, nf + f)),
            pl.BlockSpec((conv_w.shape[0], tf), lambda b, i, f: (0, f)),
            pl.BlockSpec((1, tf), lambda b, i, f: (0, f)),
            pl.BlockSpec((tf, d), lambda b, i, f: (f, 0)),
            pl.BlockSpec((1, d), lambda b, i, f: (0, 0)),
        ],
        out_specs=pl.BlockSpec((t, d), row),
        out_shape=jax.ShapeDtypeStruct((n, d), F32),
        scratch_shapes=[
            pltpu.VMEM((t, d), BF16),
            pltpu.VMEM((t, d), F32),
            pltpu.VMEM((CONV_HALO + t, tf), F32),
            pltpu.VMEM((nf, CONV_HALO, tf), F32),
        ],
        compiler_params=pltpu.CompilerParams(
            dimension_semantics=("arbitrary", "arbitrary", "arbitrary"),
            vmem_limit_bytes=VMEM_LIMIT_BYTES),
    )(x, g, w_up, w_up, conv_w, conv_b, w_down, g_final)


def kernel(x, g_mix, w_in, b_gate, w_pool, pool_scale, lru_conv_w, lru_conv_b, w_a, b_a, w_i,
           b_i, lru_lambda, w_pool_proj, w_lru_proj, w_out, g_mlp, w_up, ffn_conv_w, ffn_conv_b,
           w_down, g_final):
    batch, seq, d = x.shape
    depth = w_in.shape[0]
    pool_width = pool_scale.shape[1]
    lru_width = lru_lambda.shape[1]
    mix_width = pool_width + lru_width
    row2 = lambda v: v.reshape(1, -1)
    xf = x.reshape(batch * seq, d)
    for l in range(depth):
        d_pool, v, h, *w_gate_bf, wdn_bf = _mixproj(
            xf, row2(g_mix[l]), _cast_first_columns(w_in[l], mix_width), w_in[l], lru_conv_w[l],
            row2(lru_conv_b[l]), w_down[l], pool_width, seq)
        act_gelu, gates, wup_bf = _gateproj(h, w_gate_bf, row2(b_gate[l]), w_up[l], lru_width)
        y_pool, y_lru, wpp_bf, wlp_bf, wout_bf = _mixer(
            d_pool, v, act_gelu, w_pool[l].astype(BF16), row2(pool_scale[l]),
            w_a[l].astype(BF16), row2(b_a[l]), w_i[l].astype(BF16), row2(b_i[l]),
            row2(lru_lambda[l]), (w_pool_proj[l], w_lru_proj[l], w_out[l]), batch, seq)
        xf = _merge(y_pool, y_lru, gates, xf, wpp_bf, wlp_bf, wout_bf)
        xf = _ffn(xf, row2(g_mlp[l]), wup_bf, ffn_conv_w[l], row2(ffn_conv_b[l]),
                  wdn_bf, row2(g_final), l == depth - 1, batch, seq)
    return xf.reshape(batch, seq, d)
```

```python
import functools

import jax
import jax.numpy as jnp
from jax import lax
from jax.experimental import pallas as pl
from jax.experimental.pallas import tpu as pltpu

F32 = jnp.float32
BF16 = jnp.bfloat16

POOL_WINDOWS = (2, 4, 8, 16)
POOL_GROUP_WIDTH = 256
LRU_BLOCK_WIDTH = 256
LRU_C = 8.0
EPS = 1e-6

SUBLANES = 8
BF16_SUBLANES = 16
POOL_SUB = 4
POOL_SUB_HALO = 16
SEQ_HALO = 24
CONV_HALO = SUBLANES
VMEM_LIMIT_BYTES = 56 * 1024 * 1024

CAST_ROWS = 256
ROW_SUB = 256
PROJ_TN = 1024
MIXPROJ_TM = 512
GATEPROJ_TM = 512
GATEPROJ_SUB = 256
GATEPROJ_GROUP = 2
MIXER_T = 512
SCAN_UNROLL = 8
MERGE_TM = 512
MERGE_CHUNK = 512
FFN_T = 512
FFN_TF = 1024


def _rms_norm(x, g):
    ms = jnp.mean(x * x, axis=-1, keepdims=True)
    return x * lax.rsqrt(ms + EPS) * g


def _dot(a, b):
    return jnp.dot(a, b, preferred_element_type=F32)


def _sigmoid(x):
    return 0.5 * jnp.tanh(0.5 * x) + 0.5


def _resident(shape):
    zeros = (0,) * len(shape)
    return pl.BlockSpec(shape, lambda *_: zeros, pipeline_mode=pl.Buffered(1))


def _slab_specs(weights, n_steps, step_index):
    specs = []
    for w in weights:
        rows, cols = w.shape
        assert rows % (n_steps * BF16_SUBLANES) == 0
        specs.append(pl.BlockSpec((rows // n_steps, cols), step_index))
    return specs


def _cast_slabs(src_refs, dst_refs):
    for src, dst in zip(src_refs, dst_refs):
        dst[...] = src[...].astype(BF16)


def _cast_kernel(src_ref, dst_ref):
    dst_ref[...] = src_ref[...].astype(BF16)


def _cast_first_columns(w, width):
    rows = w.shape[0]
    assert rows % CAST_ROWS == 0 and w.shape[1] % width == 0
    return pl.pallas_call(
        _cast_kernel,
        grid=(rows // CAST_ROWS,),
        in_specs=[pl.BlockSpec((CAST_ROWS, width), lambda i: (i, 0))],
        out_specs=pl.BlockSpec((CAST_ROWS, width), lambda i: (i, 0)),
        out_shape=jax.ShapeDtypeStruct((rows, width), BF16),
    )(w)


def _mixproj_kernel(blocks_per_seq, n_cast, x_ref, g_ref, w_ref, convw_ref, convb_ref, *refs):
    cast_refs = refs[:n_cast]
    d_ref, v_ref, hout_ref = refs[n_cast:n_cast + 3]
    cast_bf_refs = refs[n_cast + 3:2 * n_cast + 3]
    h_ref, ext_ref, sub_ref, carry_ref = refs[2 * n_cast + 3:]
    _cast_slabs(cast_refs, cast_bf_refs)
    tn = PROJ_TN
    n_sub = x_ref.shape[0] // ROW_SUB
    block_in_seq = lax.rem(pl.program_id(0), blocks_per_seq)

    @pl.when(block_in_seq == 0)
    def _():
        carry_ref[...] = jnp.zeros_like(carry_ref)

    @pl.loop(0, n_sub)
    def _(r):
        rows = pl.ds(pl.multiple_of(r * ROW_SUB, ROW_SUB), ROW_SUB)
        h = _rms_norm(x_ref[rows, :], g_ref[...]).astype(BF16)
        h_ref[rows, :] = h
        hout_ref[rows, :] = h

    def matmuls(tile):
        return [_dot(h_ref[r * ROW_SUB:(r + 1) * ROW_SUB, :], w_ref[:, tile * tn:(tile + 1) * tn])
                for r in range(n_sub)]

    def place(tile, r, p):
        slot = r % 2
        if r == 0:
            ext_ref[slot, 0:SEQ_HALO, :] = carry_ref[tile]
        ext_ref[slot, SEQ_HALO:SEQ_HALO + ROW_SUB, :] = p
        return slot

    def pass_tail(tile, r):
        tail = ext_ref[r % 2, ROW_SUB:ROW_SUB + SEQ_HALO, :]
        if r + 1 < n_sub:
            ext_ref[(r + 1) % 2, 0:SEQ_HALO, :] = tail
        else:
            carry_ref[tile] = tail

    def window(load, q0, n, count, stride):
        s = load(q0, n)
        for m in range(1, count):
            s = s + load(q0 - m * stride, n)
        return s

    wide = [g for g, w in enumerate(POOL_WINDOWS) if w > POOL_SUB]
    all_ps = [matmuls(tile) for tile in range(w_ref.shape[1] // tn)]
    ps = all_ps[0]
    for r in range(n_sub):
        rows = slice(r * ROW_SUB, (r + 1) * ROW_SUB)
        slot = place(0, r, ps[r])
        pos = (block_in_seq * x_ref.shape[0] + r * ROW_SUB + 1
               + lax.broadcasted_iota(jnp.int32, (ROW_SUB, 1), 0)).astype(F32)
        for g, w in enumerate(POOL_WINDOWS):
            cols = slice(g * POOL_GROUP_WIDTH, (g + 1) * POOL_GROUP_WIDTH)
            load_ext = lambda q0, n: ext_ref[slot, q0:q0 + n, cols]
            if w <= POOL_SUB:
                s = window(load_ext, SEQ_HALO, ROW_SUB, w, 1)
            else:
                scols = slice(wide.index(g) * POOL_GROUP_WIDTH,
                              (wide.index(g) + 1) * POOL_GROUP_WIDTH)
                n = POOL_SUB_HALO + ROW_SUB
                sub_ref[slot, 0:n, scols] = window(
                    load_ext, SEQ_HALO - POOL_SUB_HALO, n, POOL_SUB, 1)
                load_sub = lambda q0, n: sub_ref[slot, q0:q0 + n, scols]
                s = window(load_sub, POOL_SUB_HALO, ROW_SUB, w // POOL_SUB, POOL_SUB)
            inv_count = 1.0 / jnp.minimum(pos, float(w))
            d = s * inv_count - ext_ref[slot, SEQ_HALO:SEQ_HALO + ROW_SUB, cols]
            d_ref[rows, cols] = d.astype(BF16)
        pass_tail(0, r)

    n_taps = convw_ref.shape[0]
    for tile in range(1, w_ref.shape[1] // tn):
        cols = slice((tile - 1) * tn, tile * tn)
        ps = all_ps[tile]
        for r in range(n_sub):
            slot = place(tile, r, ps[r])
            v = convb_ref[:, cols]
            for k in range(n_taps):
                v = v + (ext_ref[slot, SEQ_HALO - k:SEQ_HALO - k + ROW_SUB, :]
                         * convw_ref[n_taps - 1 - k:n_taps - k, cols])
            v_ref[r * ROW_SUB:(r + 1) * ROW_SUB, cols] = v
            pass_tail(tile, r)


def _mixproj(x, g, w_mix, w_in, conv_w, conv_b, cast_weight, pool_width, seq):
    n, d = x.shape
    mix_width = w_mix.shape[1]
    lru_width = mix_width - pool_width
    tm, tn = MIXPROJ_TM, PROJ_TN
    assert pool_width == tn == len(POOL_WINDOWS) * POOL_GROUP_WIDTH and lru_width % tn == 0
    assert n % tm == 0 and seq % tm == 0 and tm % ROW_SUB == 0
    n_wide = sum(w > POOL_SUB for w in POOL_WINDOWS)
    n_steps = n // tm
    row = lambda i: (i, 0)
    assert w_in.shape[1] % mix_width == 0 and d % (n_steps * BF16_SUBLANES) == 0
    n_gblocks = w_in.shape[1] // mix_width - 1
    gate_in_specs = [pl.BlockSpec((d // n_steps, mix_width),
                                  functools.partial(lambda k, i: (i, k), 1 + k))
                     for k in range(n_gblocks)]
    gate_out_specs = [pl.BlockSpec((d // n_steps, mix_width), row)] * n_gblocks
    return pl.pallas_call(
        functools.partial(_mixproj_kernel, seq // tm, n_gblocks + 1),
        grid=(n_steps,),
        in_specs=[
            pl.BlockSpec((tm, d), row),
            _resident((1, d)), _resident(w_mix.shape),
            _resident(conv_w.shape), _resident(conv_b.shape),
            *gate_in_specs, *_slab_specs((cast_weight,), n_steps, row),
        ],
        out_specs=[pl.BlockSpec((tm, pool_width), row), pl.BlockSpec((tm, lru_width), row),
                   pl.BlockSpec((tm, d), row),
                   *gate_out_specs, *_slab_specs((cast_weight,), n_steps, row)],
        out_shape=[
            jax.ShapeDtypeStruct((n, pool_width), BF16),
            jax.ShapeDtypeStruct((n, lru_width), F32),
            jax.ShapeDtypeStruct((n, d), BF16),
            *[jax.ShapeDtypeStruct((d, mix_width), BF16)] * n_gblocks,
            jax.ShapeDtypeStruct(cast_weight.shape, BF16),
        ],
        scratch_shapes=[
            pltpu.VMEM((tm, d), BF16),
            pltpu.VMEM((2, SEQ_HALO + ROW_SUB, tn), F32),
            pltpu.VMEM((2, POOL_SUB_HALO + ROW_SUB, n_wide * POOL_GROUP_WIDTH), F32),
            pltpu.VMEM((mix_width // tn, SEQ_HALO, tn), F32),
        ],
        compiler_params=pltpu.CompilerParams(
            dimension_semantics=("arbitrary",),
            vmem_limit_bytes=VMEM_LIMIT_BYTES),
    )(x, g, w_mix, conv_w, conv_b, *([w_in] * n_gblocks), cast_weight)


def _gateproj_kernel(n_gelu, n_wblocks, h_ref, *refs):
    w_refs = refs[:n_wblocks]
    b_ref, cast_ref, gelu_ref, gates_ref, cast_bf_ref = refs[n_wblocks:]
    _cast_slabs((cast_ref,), (cast_bf_ref,))
    tn = PROJ_TN
    tiles_per_block = w_refs[0].shape[1] // tn
    n_sub = h_ref.shape[0] // GATEPROJ_SUB
    n_tiles = n_wblocks * tiles_per_block

    def project(tile, r):
        w_ref = w_refs[tile // tiles_per_block]
        wcols = slice((tile % tiles_per_block) * tn, (tile % tiles_per_block + 1) * tn)
        return _dot(h_ref[r * GATEPROJ_SUB:(r + 1) * GATEPROJ_SUB, :], w_ref[:, wcols])

    def finish(tile, r, p):
        rows = slice(r * GATEPROJ_SUB, (r + 1) * GATEPROJ_SUB)
        if tile < n_gelu:
            gelu_ref[rows, tile * tn:(tile + 1) * tn] = jax.nn.gelu(p).astype(BF16)
        else:
            cols = slice((tile - n_gelu) * tn, (tile - n_gelu + 1) * tn)
            gates_ref[rows, cols] = jax.nn.sigmoid(p + b_ref[:, cols]).astype(BF16)

    for t0 in range(0, n_tiles, GATEPROJ_GROUP):
        group = [(tile, r) for tile in range(t0, min(t0 + GATEPROJ_GROUP, n_tiles))
                 for r in range(n_sub)]
        ps = [project(tile, r) for tile, r in group]
        for (tile, r), p in zip(group, ps):
            finish(tile, r, p)


def _gateproj(h, w_blocks, b_gate, cast_weight, gelu_width):
    n, d = h.shape
    n_wblocks = len(w_blocks)
    gate_width = sum(w.shape[1] for w in w_blocks) - gelu_width
    tm, tn = GATEPROJ_TM, PROJ_TN
    assert n % tm == 0 and gelu_width % tn == 0 and gate_width % tn == 0
    assert all(w.shape == w_blocks[0].shape for w in w_blocks) and w_blocks[0].shape[1] % tn == 0
    row = lambda i: (i, 0)
    w_specs = [_resident(w.shape) for w in w_blocks]
    return pl.pallas_call(
        functools.partial(_gateproj_kernel, gelu_width // tn, n_wblocks),
        grid=(n // tm,),
        in_specs=[pl.BlockSpec((tm, d), row), *w_specs, _resident(b_gate.shape),
                  *_slab_specs((cast_weight,), n // tm, row)],
        out_specs=[pl.BlockSpec((tm, gelu_width), row), pl.BlockSpec((tm, gate_width), row),
                   *_slab_specs((cast_weight,), n // tm, row)],
        out_shape=[jax.ShapeDtypeStruct((n, gelu_width), BF16),
                   jax.ShapeDtypeStruct((n, gate_width), BF16),
                   jax.ShapeDtypeStruct(cast_weight.shape, BF16)],
        compiler_params=pltpu.CompilerParams(
            dimension_semantics=("arbitrary",),
            vmem_limit_bytes=VMEM_LIMIT_BYTES),
    )(h, *w_blocks, b_gate, cast_weight)


def _mixer_kernel(d_ref, v_ref, gelu_ref, wpool_ref, pscale_ref,
                  wa_ref, ba_ref, wi_ref, bi_ref, lam_ref, wpp_ref, wlp_ref, wout_ref,
                  ypool_ref, ylru_ref, wpp_bf_ref, wlp_bf_ref, wout_bf_ref,
                  a_ref, bx_ref, h_ref, hcarry_ref):
    _cast_slabs((wpp_ref, wlp_ref, wout_ref), (wpp_bf_ref, wlp_bf_ref, wout_bf_ref))

    t_rows = ylru_ref.shape[0]
    lru_width = ylru_ref.shape[1]

    @pl.when(pl.program_id(1) == 0)
    def _():
        hcarry_ref[...] = jnp.zeros_like(hcarry_ref)

    for g in range(len(POOL_WINDOWS)):
        cols = slice(g * POOL_GROUP_WIDTH, (g + 1) * POOL_GROUP_WIDTH)
        y = _dot(d_ref[:, cols], wpool_ref[g]) * pscale_ref[:, cols]
        ypool_ref[:, cols] = y.astype(BF16)

    half_log_a_scale = (-0.5 * LRU_C) * jax.nn.softplus(-lam_ref[...])
    for hd in range(lru_width // LRU_BLOCK_WIDTH):
        cols = slice(hd * LRU_BLOCK_WIDTH, (hd + 1) * LRU_BLOCK_WIDTH)
        v = v_ref[:, cols]
        vb = v.astype(BF16)
        x_r = _dot(vb, wa_ref[hd]) + ba_ref[:, cols]
        ig = _sigmoid(_dot(vb, wi_ref[hd]) + bi_ref[:, cols])
        log_a = (jnp.tanh(0.5 * x_r) + 1.0) * half_log_a_scale[:, cols]
        a = jnp.exp(log_a)
        q = -jnp.tanh(log_a) * (a * a + 1.0)
        mult = jnp.where(q > 0.0, q * lax.rsqrt(q), 0.0)
        a_ref[:, cols] = a
        bx_ref[:, cols] = mult * (ig * v)

    def step(t, h):
        h = a_ref[pl.ds(t, 1), :] * h + bx_ref[pl.ds(t, 1), :]
        h_ref[pl.ds(t, 1), :] = h
        return h

    hcarry_ref[...] = lax.fori_loop(0, t_rows, step, hcarry_ref[...], unroll=SCAN_UNROLL)
    ylru_ref[...] = (h_ref[...] * gelu_ref[...].astype(F32)).astype(BF16)


def _mixer(d_pool, v, act_gelu, w_pool, pool_scale, w_a, b_a, w_i, b_i, lam,
           cast_weights, batch, seq):
    n, pool_width = d_pool.shape
    lru_width = v.shape[1]
    t = MIXER_T
    assert seq % t == 0
    nblk = seq // t
    row = lambda b, i: (b * nblk + i, 0)
    return pl.pallas_call(
        _mixer_kernel,
        grid=(batch, nblk),
        in_specs=[
            pl.BlockSpec((t, pool_width), row),
            pl.BlockSpec((t, lru_width), row),
            pl.BlockSpec((t, lru_width), row),
            _resident(w_pool.shape), _resident(pool_scale.shape),
            _resident(w_a.shape), _resident(b_a.shape),
            _resident(w_i.shape), _resident(b_i.shape), _resident(lam.shape),
            *_slab_specs(cast_weights, batch * nblk, row),
        ],
        out_specs=[pl.BlockSpec((t, pool_width), row), pl.BlockSpec((t, lru_width), row),
                   *_slab_specs(cast_weights, batch * nblk, row)],
        out_shape=[jax.ShapeDtypeStruct((n, pool_width), BF16),
                   jax.ShapeDtypeStruct((n, lru_width), BF16),
                   *[jax.ShapeDtypeStruct(w.shape, BF16) for w in cast_weights]],
        scratch_shapes=[
            pltpu.VMEM((t, lru_width), F32),
            pltpu.VMEM((t, lru_width), F32),
            pltpu.VMEM((t, lru_width), F32),
            pltpu.VMEM((1, lru_width), F32),
        ],
        compiler_params=pltpu.CompilerParams(
            dimension_semantics=("arbitrary", "arbitrary"),
            vmem_limit_bytes=VMEM_LIMIT_BYTES),
    )(d_pool, v, act_gelu, w_pool, pool_scale, w_a, b_a, w_i, b_i, lam, *cast_weights)


def _merge_kernel(yp_ref, yl_ref, gates_ref, x_ref, wpp_ref, wlp_ref, wout_ref, o_ref, m_ref):
    d = o_ref.shape[1]
    ch = MERGE_CHUNK
    for c in range(d // ch):
        cols = slice(c * ch, (c + 1) * ch)
        p = _dot(yp_ref[...], wpp_ref[:, cols])
        q = _dot(yl_ref[...], wlp_ref[:, cols])
        g_pool = gates_ref[:, cols].astype(F32)
        g_lru = gates_ref[:, d + c * ch:d + (c + 1) * ch].astype(F32)
        m_ref[:, cols] = (g_pool * p + g_lru * q).astype(BF16)
    for c in range(d // ch):
        cols = slice(c * ch, (c + 1) * ch)
        o_ref[:, cols] = x_ref[:, cols] + _dot(m_ref[...], wout_ref[:, cols])


def _merge(y_pool, y_lru, gates, x, w_pool_proj, w_lru_proj, w_out):
    n, d = x.shape
    tm = MERGE_TM
    assert n % tm == 0 and d % MERGE_CHUNK == 0
    row = lambda i: (i, 0)
    return pl.pallas_call(
        _merge_kernel,
        grid=(n // tm,),
        in_specs=[
            pl.BlockSpec((tm, y_pool.shape[1]), row),
            pl.BlockSpec((tm, y_lru.shape[1]), row),
            pl.BlockSpec((tm, gates.shape[1]), row),
            pl.BlockSpec((tm, d), row),
            _resident(w_pool_proj.shape), _resident(w_lru_proj.shape), _resident(w_out.shape),
        ],
        out_specs=pl.BlockSpec((tm, d), row),
        out_shape=jax.ShapeDtypeStruct((n, d), F32),
        scratch_shapes=[pltpu.VMEM((tm, d), BF16)],
        compiler_params=pltpu.CompilerParams(
            dimension_semantics=("arbitrary",),
            vmem_limit_bytes=VMEM_LIMIT_BYTES),
    )(y_pool, y_lru, gates, x, w_pool_proj, w_lru_proj, w_out)


def _ffn_kernel(apply_final, x_ref, g_ref, wg_ref, wv_ref, cw_ref, cb_ref, wd_ref, gfin_ref,
                o_ref, h2_ref, ext_ref, carry_ref):
    i = pl.program_id(1)
    f = pl.program_id(2)
    t_rows = x_ref.shape[0]

    @pl.when(f == 0)
    def _():
        x = x_ref[...]
        h2_ref[...] = _rms_norm(x, g_ref[...]).astype(BF16)
        o_ref[...] = x

    @pl.when(i == 0)
    def _():
        ext_ref[0:CONV_HALO, :] = jnp.zeros((CONV_HALO, ext_ref.shape[1]), F32)

    @pl.when(i > 0)
    def _():
        ext_ref[0:CONV_HALO, :] = carry_ref[f]

    n_taps = cw_ref.shape[0]
    n_sub = t_rows // ROW_SUB
    ups = [(_dot(h2_ref[r * ROW_SUB:(r + 1) * ROW_SUB, :], wg_ref[...]),
            _dot(h2_ref[r * ROW_SUB:(r + 1) * ROW_SUB, :], wv_ref[...])) for r in range(n_sub)]
    for r in range(n_sub):
        rows = slice(r * ROW_SUB, (r + 1) * ROW_SUB)
        base = CONV_HALO + r * ROW_SUB
        gate_pre, val = ups[r]
        ext_ref[base:base + ROW_SUB, :] = gate_pre
        conv = cb_ref[...]
        for k in range(n_taps):
            conv = conv + (ext_ref[base - k:base - k + ROW_SUB, :]
                           * cw_ref[n_taps - 1 - k:n_taps - k, :])
        act = (jax.nn.gelu(conv) * val).astype(BF16)
        o_ref[rows, :] += _dot(act, wd_ref[...])

    carry_ref[f] = ext_ref[t_rows:t_rows + CONV_HALO, :]

    if apply_final:
        @pl.when(f == pl.num_programs(2) - 1)
        def _():
            o_ref[...] = _rms_norm(o_ref[...], gfin_ref[...])


def _ffn(x, g, w_up, conv_w, conv_b, w_down, g_final, apply_final, batch, seq):
    n, d = x.shape
    d_ff = w_down.shape[0]
    t, tf = FFN_T, FFN_TF
    assert seq % t == 0 and d_ff % tf == 0
    nblk, nf = seq // t, d_ff // tf
    row = lambda b, i, f: (b * nblk + i, 0)
    return pl.pallas_call(
        functools.partial(_ffn_kernel, apply_final),
        grid=(batch, nblk, nf),
        in_specs=[
            pl.BlockSpec((t, d), row),
            pl.BlockSpec((1, d), lambda b, i, f: (0, 0)),
            pl.BlockSpec((d, tf), lambda b, i, f: (0, f)),
            pl.BlockSpec((d, tf), lambda b, i, f: (0, nf + f)),
            pl.BlockSpec((conv_w.shape[0], tf), lambda b, i, f: (0, f)),
            pl.BlockSpec((1, tf), lambda b, i, f: (0, f)),
            pl.BlockSpec((tf, d), lambda b, i, f: (f, 0)),
            pl.BlockSpec((1, d), lambda b, i, f: (0, 0)),
        ],
        out_specs=pl.BlockSpec((t, d), row),
        out_shape=jax.ShapeDtypeStruct((n, d), F32),
        scratch_shapes=[
            pltpu.VMEM((t, d), BF16),
            pltpu.VMEM((CONV_HALO + t, tf), F32),
            pltpu.VMEM((nf, CONV_HALO, tf), F32),
        ],
        compiler_params=pltpu.CompilerParams(
            dimension_semantics=("arbitrary", "arbitrary", "arbitrary"),
            vmem_limit_bytes=VMEM_LIMIT_BYTES),
    )(x, g, w_up, w_up, conv_w, conv_b, w_down, g_final)


def kernel(x, g_mix, w_in, b_gate, w_pool, pool_scale, lru_conv_w, lru_conv_b, w_a, b_a, w_i,
           b_i, lru_lambda, w_pool_proj, w_lru_proj, w_out, g_mlp, w_up, ffn_conv_w, ffn_conv_b,
           w_down, g_final):
    batch, seq, d = x.shape
    depth = w_in.shape[0]
    pool_width = pool_scale.shape[1]
    lru_width = lru_lambda.shape[1]
    mix_width = pool_width + lru_width
    row2 = lambda v: v.reshape(1, -1)
    xf = x.reshape(batch * seq, d)
    for l in range(depth):
        d_pool, v, h, *w_gate_bf, wdn_bf = _mixproj(
            xf, row2(g_mix[l]), _cast_first_columns(w_in[l], mix_width), w_in[l], lru_conv_w[l],
            row2(lru_conv_b[l]), w_down[l], pool_width, seq)
        act_gelu, gates, wup_bf = _gateproj(h, w_gate_bf, row2(b_gate[l]), w_up[l], lru_width)
        y_pool, y_lru, wpp_bf, wlp_bf, wout_bf = _mixer(
            d_pool, v, act_gelu, w_pool[l].astype(BF16), row2(pool_scale[l]),
            w_a[l].astype(BF16), row2(b_a[l]), w_i[l].astype(BF16), row2(b_i[l]),
            row2(lru_lambda[l]), (w_pool_proj[l], w_lru_proj[l], w_out[l]), batch, seq)
        xf = _merge(y_pool, y_lru, gates, xf, wpp_bf, wlp_bf, wout_bf)
        xf = _ffn(xf, row2(g_mlp[l]), wup_bf, ffn_conv_w[l], row2(ffn_conv_b[l]),
                  wdn_bf, row2(g_final), l == depth - 1, batch, seq)
    return xf.reshape(batch, seq, d)
```

```python
import functools

import jax
import jax.numpy as jnp
from jax import lax
from jax.experimental import pallas as pl
from jax.experimental.pallas import tpu as pltpu

F32 = jnp.float32
BF16 = jnp.bfloat16

POOL_WINDOWS = (2, 4, 8, 16)
POOL_GROUP_WIDTH = 256
LRU_BLOCK_WIDTH = 256
LRU_C = 8.0
EPS = 1e-6

SUBLANES = 8
BF16_SUBLANES = 16
POOL_SUB = 4
POOL_SUB_HALO = 16
SEQ_HALO = 24
CONV_HALO = SUBLANES
VMEM_LIMIT_BYTES = 56 * 1024 * 1024

CAST_ROWS = 256
ROW_SUB = 256
PROJ_TN = 1024
MIXPROJ_TM = 512
GATEPROJ_TM = 256
GATEPROJ_GROUP = 2
MIXER_T = 512
MERGE_TM = 512
MERGE_CHUNK = 512
FFN_T = 512
FFN_TF = 1024


def _rms_norm(x, g):
    ms = jnp.mean(x * x, axis=-1, keepdims=True)
    return x * lax.rsqrt(ms + EPS) * g


def _dot(a, b):
    return jnp.dot(a, b, preferred_element_type=F32)


def _sigmoid(x):
    return 0.5 * jnp.tanh(0.5 * x) + 0.5


def _resident(shape):
    zeros = (0,) * len(shape)
    return pl.BlockSpec(shape, lambda *_: zeros, pipeline_mode=pl.Buffered(1))


def _slab_specs(weights, n_steps, step_index):
    specs = []
    for w in weights:
        rows, cols = w.shape
        assert rows % (n_steps * BF16_SUBLANES) == 0
        specs.append(pl.BlockSpec((rows // n_steps, cols), step_index))
    return specs


def _cast_slabs(src_refs, dst_refs):
    for src, dst in zip(src_refs, dst_refs):
        dst[...] = src[...].astype(BF16)


def _cast_kernel(src_ref, dst_ref):
    dst_ref[...] = src_ref[...].astype(BF16)


def _cast_first_columns(w, width):
    rows = w.shape[0]
    assert rows % CAST_ROWS == 0 and w.shape[1] % width == 0
    return pl.pallas_call(
        _cast_kernel,
        grid=(rows // CAST_ROWS,),
        in_specs=[pl.BlockSpec((CAST_ROWS, width), lambda i: (i, 0))],
        out_specs=pl.BlockSpec((CAST_ROWS, width), lambda i: (i, 0)),
        out_shape=jax.ShapeDtypeStruct((rows, width), BF16),
    )(w)


def _mixproj_kernel(blocks_per_seq, n_cast, x_ref, g_ref, w_ref, convw_ref, convb_ref, *refs):
    cast_refs = refs[:n_cast]
    d_ref, v_ref, hout_ref = refs[n_cast:n_cast + 3]
    cast_bf_refs = refs[n_cast + 3:2 * n_cast + 3]
    h_ref, ext_ref, sub_ref, carry_ref = refs[2 * n_cast + 3:]
    _cast_slabs(cast_refs, cast_bf_refs)
    tn = PROJ_TN
    n_sub = x_ref.shape[0] // ROW_SUB
    block_in_seq = lax.rem(pl.program_id(0), blocks_per_seq)

    @pl.when(block_in_seq == 0)
    def _():
        carry_ref[...] = jnp.zeros_like(carry_ref)

    @pl.loop(0, n_sub)
    def _(r):
        rows = pl.ds(pl.multiple_of(r * ROW_SUB, ROW_SUB), ROW_SUB)
        h = _rms_norm(x_ref[rows, :], g_ref[...]).astype(BF16)
        h_ref[rows, :] = h
        hout_ref[rows, :] = h

    def matmuls(tile):
        return [_dot(h_ref[r * ROW_SUB:(r + 1) * ROW_SUB, :], w_ref[:, tile * tn:(tile + 1) * tn])
                for r in range(n_sub)]

    def place(tile, r, p):
        slot = r % 2
        if r == 0:
            ext_ref[slot, 0:SEQ_HALO, :] = carry_ref[tile]
        ext_ref[slot, SEQ_HALO:SEQ_HALO + ROW_SUB, :] = p
        return slot

    def pass_tail(tile, r):
        tail = ext_ref[r % 2, ROW_SUB:ROW_SUB + SEQ_HALO, :]
        if r + 1 < n_sub:
            ext_ref[(r + 1) % 2, 0:SEQ_HALO, :] = tail
        else:
            carry_ref[tile] = tail

    def window(load, q0, n, count, stride):
        s = load(q0, n)
        for m in range(1, count):
            s = s + load(q0 - m * stride, n)
        return s

    wide = [g for g, w in enumerate(POOL_WINDOWS) if w > POOL_SUB]
    all_ps = [matmuls(tile) for tile in range(w_ref.shape[1] // tn)]
    ps = all_ps[0]
    for r in range(n_sub):
        rows = slice(r * ROW_SUB, (r + 1) * ROW_SUB)
        slot = place(0, r, ps[r])
        pos = (block_in_seq * x_ref.shape[0] + r * ROW_SUB + 1
               + lax.broadcasted_iota(jnp.int32, (ROW_SUB, 1), 0)).astype(F32)
        for g, w in enumerate(POOL_WINDOWS):
            cols = slice(g * POOL_GROUP_WIDTH, (g + 1) * POOL_GROUP_WIDTH)
            load_ext = lambda q0, n: ext_ref[slot, q0:q0 + n, cols]
            if w <= POOL_SUB:
                s = window(load_ext, SEQ_HALO, ROW_SUB, w, 1)
            else:
                scols = slice(wide.index(g) * POOL_GROUP_WIDTH,
                              (wide.index(g) + 1) * POOL_GROUP_WIDTH)
                n = POOL_SUB_HALO + ROW_SUB
                sub_ref[slot, 0:n, scols] = window(
                    load_ext, SEQ_HALO - POOL_SUB_HALO, n, POOL_SUB, 1)
                load_sub = lambda q0, n: sub_ref[slot, q0:q0 + n, scols]
                s = window(load_sub, POOL_SUB_HALO, ROW_SUB, w // POOL_SUB, POOL_SUB)
            inv_count = 1.0 / jnp.minimum(pos, float(w))
            d = s * inv_count - ext_ref[slot, SEQ_HALO:SEQ_HALO + ROW_SUB, cols]
            d_ref[rows, cols] = d.astype(BF16)
        pass_tail(0, r)

    n_taps = convw_ref.shape[0]
    for tile in range(1, w_ref.shape[1] // tn):
        cols = slice((tile - 1) * tn, tile * tn)
        ps = all_ps[tile]
        for r in range(n_sub):
            slot = place(tile, r, ps[r])
            v = convb_ref[:, cols]
            for k in range(n_taps):
                v = v + (ext_ref[slot, SEQ_HALO - k:SEQ_HALO - k + ROW_SUB, :]
                         * convw_ref[n_taps - 1 - k:n_taps - k, cols])
            v_ref[r * ROW_SUB:(r + 1) * ROW_SUB, cols] = v
            pass_tail(tile, r)


def _mixproj(x, g, w_mix, w_in, conv_w, conv_b, cast_weight, pool_width, seq):
    n, d = x.shape
    mix_width = w_mix.shape[1]
    lru_width = mix_width - pool_width
    tm, tn = MIXPROJ_TM, PROJ_TN
    assert pool_width == tn == len(POOL_WINDOWS) * POOL_GROUP_WIDTH and lru_width % tn == 0
    assert n % tm == 0 and seq % tm == 0 and tm % ROW_SUB == 0
    n_wide = sum(w > POOL_SUB for w in POOL_WINDOWS)
    n_steps = n // tm
    row = lambda i: (i, 0)
    assert w_in.shape[1] % mix_width == 0 and d % (n_steps * BF16_SUBLANES) == 0
    n_gblocks = w_in.shape[1] // mix_width - 1
    gate_in_specs = [pl.BlockSpec((d // n_steps, mix_width),
                                  functools.partial(lambda k, i: (i, k), 1 + k))
                     for k in range(n_gblocks)]
    gate_out_specs = [pl.BlockSpec((d // n_steps, mix_width), row)] * n_gblocks
    return pl.pallas_call(
        functools.partial(_mixproj_kernel, seq // tm, n_gblocks + 1),
        grid=(n_steps,),
        in_specs=[
            pl.BlockSpec((tm, d), row),
            _resident((1, d)), _resident(w_mix.shape),
            _resident(conv_w.shape), _resident(conv_b.shape),
            *gate_in_specs, *_slab_specs((cast_weight,), n_steps, row),
        ],
        out_specs=[pl.BlockSpec((tm, pool_width), row), pl.BlockSpec((tm, lru_width), row),
                   pl.BlockSpec((tm, d), row),
                   *gate_out_specs, *_slab_specs((cast_weight,), n_steps, row)],
        out_shape=[
            jax.ShapeDtypeStruct((n, pool_width), BF16),
            jax.ShapeDtypeStruct((n, lru_width), F32),
            jax.ShapeDtypeStruct((n, d), BF16),
            *[jax.ShapeDtypeStruct((d, mix_width), BF16)] * n_gblocks,
            jax.ShapeDtypeStruct(cast_weight.shape, BF16),
        ],
        scratch_shapes=[
            pltpu.VMEM((tm, d), BF16),
            pltpu.VMEM((2, SEQ_HALO + ROW_SUB, tn), F32),
            pltpu.VMEM((2, POOL_SUB_HALO + ROW_SUB, n_wide * POOL_GROUP_WIDTH), F32),
            pltpu.VMEM((mix_width // tn, SEQ_HALO, tn), F32),
        ],
        compiler_params=pltpu.CompilerParams(
            dimension_semantics=("arbitrary",),
            vmem_limit_bytes=VMEM_LIMIT_BYTES),
    )(x, g, w_mix, conv_w, conv_b, *([w_in] * n_gblocks), cast_weight)


def _gateproj_kernel(n_gelu, n_wblocks, blocks_per_seq, h_ref, a_ref, bx_ref, *refs):
    w_refs = refs[:n_wblocks]
    (b_ref, cast_ref, ylru_ref, gates_ref, cast_bf_ref,
     gelu_ref, hs_ref, hcarry_ref) = refs[n_wblocks:]
    _cast_slabs((cast_ref,), (cast_bf_ref,))
    t_rows = h_ref.shape[0]
    tn = PROJ_TN
    tiles_per_block = w_refs[0].shape[1] // tn
    n_tiles = n_wblocks * tiles_per_block

    @pl.when(lax.rem(pl.program_id(0), blocks_per_seq) == 0)
    def _():
        hcarry_ref[...] = jnp.zeros_like(hcarry_ref)

    h = hcarry_ref[...]
    for t in range(t_rows):
        h = a_ref[t:t + 1, :] * h + bx_ref[t:t + 1, :]
        hs_ref[t:t + 1, :] = h
    hcarry_ref[...] = h

    def project(tile):
        w_ref = w_refs[tile // tiles_per_block]
        wcols = slice((tile % tiles_per_block) * tn, (tile % tiles_per_block + 1) * tn)
        return _dot(h_ref[...], w_ref[:, wcols])

    def finish(tile, p):
        if tile < n_gelu:
            gelu_ref[:, tile * tn:(tile + 1) * tn] = jax.nn.gelu(p)
        else:
            cols = slice((tile - n_gelu) * tn, (tile - n_gelu + 1) * tn)
            gates_ref[:, cols] = jax.nn.sigmoid(p + b_ref[:, cols]).astype(BF16)

    for t0 in range(0, n_tiles, GATEPROJ_GROUP):
        group = range(t0, min(t0 + GATEPROJ_GROUP, n_tiles))
        ps = [project(tile) for tile in group]
        for tile, p in zip(group, ps):
            finish(tile, p)

    ylru_ref[...] = (hs_ref[...] * gelu_ref[...]).astype(BF16)


def _gateproj(h, a, bx, w_blocks, b_gate, cast_weight, seq):
    n, d = h.shape
    gelu_width = a.shape[1]
    n_wblocks = len(w_blocks)
    gate_width = sum(w.shape[1] for w in w_blocks) - gelu_width
    tm, tn = GATEPROJ_TM, PROJ_TN
    assert n % tm == 0 and seq % tm == 0 and gelu_width % tn == 0 and gate_width % tn == 0
    assert all(w.shape == w_blocks[0].shape for w in w_blocks) and w_blocks[0].shape[1] % tn == 0
    row = lambda i: (i, 0)
    w_specs = [_resident(w.shape) for w in w_blocks]
    return pl.pallas_call(
        functools.partial(_gateproj_kernel, gelu_width // tn, n_wblocks, seq // tm),
        grid=(n // tm,),
        in_specs=[pl.BlockSpec((tm, d), row), pl.BlockSpec((tm, gelu_width), row),
                  pl.BlockSpec((tm, gelu_width), row), *w_specs, _resident(b_gate.shape),
                  *_slab_specs((cast_weight,), n // tm, row)],
        out_specs=[pl.BlockSpec((tm, gelu_width), row), pl.BlockSpec((tm, gate_width), row),
                   *_slab_specs((cast_weight,), n // tm, row)],
        out_shape=[jax.ShapeDtypeStruct((n, gelu_width), BF16),
                   jax.ShapeDtypeStruct((n, gate_width), BF16),
                   jax.ShapeDtypeStruct(cast_weight.shape, BF16)],
        scratch_shapes=[
            pltpu.VMEM((tm, gelu_width), F32),
            pltpu.VMEM((tm, gelu_width), F32),
            pltpu.VMEM((1, gelu_width), F32),
        ],
        compiler_params=pltpu.CompilerParams(
            dimension_semantics=("arbitrary",),
            vmem_limit_bytes=VMEM_LIMIT_BYTES),
    )(h, a, bx, *w_blocks, b_gate, cast_weight)


def _mixer_kernel(d_ref, v_ref, wpool_ref, pscale_ref,
                  wa_ref, ba_ref, wi_ref, bi_ref, lam_ref, wpp_ref, wlp_ref, wout_ref,
                  ypool_ref, a_ref, bx_ref, wpp_bf_ref, wlp_bf_ref, wout_bf_ref):
    _cast_slabs((wpp_ref, wlp_ref, wout_ref), (wpp_bf_ref, wlp_bf_ref, wout_bf_ref))
    lru_width = a_ref.shape[1]

    for g in range(len(POOL_WINDOWS)):
        cols = slice(g * POOL_GROUP_WIDTH, (g + 1) * POOL_GROUP_WIDTH)
        y = _dot(d_ref[:, cols], wpool_ref[g]) * pscale_ref[:, cols]
        ypool_ref[:, cols] = y.astype(BF16)

    half_log_a_scale = (-0.5 * LRU_C) * jax.nn.softplus(-lam_ref[...])
    for hd in range(lru_width // LRU_BLOCK_WIDTH):
        cols = slice(hd * LRU_BLOCK_WIDTH, (hd + 1) * LRU_BLOCK_WIDTH)
        v = v_ref[:, cols]
        vb = v.astype(BF16)
        x_r = _dot(vb, wa_ref[hd]) + ba_ref[:, cols]
        ig = _sigmoid(_dot(vb, wi_ref[hd]) + bi_ref[:, cols])
        log_a = (jnp.tanh(0.5 * x_r) + 1.0) * half_log_a_scale[:, cols]
        a = jnp.exp(log_a)
        q = -jnp.tanh(log_a) * (a * a + 1.0)
        mult = jnp.where(q > 0.0, q * lax.rsqrt(q), 0.0)
        a_ref[:, cols] = a
        bx_ref[:, cols] = mult * (ig * v)


def _mixer(d_pool, v, w_pool, pool_scale, w_a, b_a, w_i, b_i, lam, cast_weights):
    n, pool_width = d_pool.shape
    lru_width = v.shape[1]
    t = MIXER_T
    assert n % t == 0
    row = lambda i: (i, 0)
    return pl.pallas_call(
        _mixer_kernel,
        grid=(n // t,),
        in_specs=[
            pl.BlockSpec((t, pool_width), row),
            pl.BlockSpec((t, lru_width), row),
            _resident(w_pool.shape), _resident(pool_scale.shape),
            _resident(w_a.shape), _resident(b_a.shape),
            _resident(w_i.shape), _resident(b_i.shape), _resident(lam.shape),
            *_slab_specs(cast_weights, n // t, row),
        ],
        out_specs=[pl.BlockSpec((t, pool_width), row), pl.BlockSpec((t, lru_width), row),
                   pl.BlockSpec((t, lru_width), row), *_slab_specs(cast_weights, n // t, row)],
        out_shape=[jax.ShapeDtypeStruct((n, pool_width), BF16),
                   jax.ShapeDtypeStruct((n, lru_width), F32),
                   jax.ShapeDtypeStruct((n, lru_width), F32),
                   *[jax.ShapeDtypeStruct(w.shape, BF16) for w in cast_weights]],
        compiler_params=pltpu.CompilerParams(
            dimension_semantics=("arbitrary",),
            vmem_limit_bytes=VMEM_LIMIT_BYTES),
    )(d_pool, v, w_pool, pool_scale, w_a, b_a, w_i, b_i, lam, *cast_weights)


def _merge_kernel(yp_ref, yl_ref, gates_ref, x_ref, wpp_ref, wlp_ref, wout_ref, o_ref, m_ref):
    d = o_ref.shape[1]
    ch = MERGE_CHUNK
    for c in range(d // ch):
        cols = slice(c * ch, (c + 1) * ch)
        p = _dot(yp_ref[...], wpp_ref[:, cols])
        q = _dot(yl_ref[...], wlp_ref[:, cols])
        g_pool = gates_ref[:, cols].astype(F32)
        g_lru = gates_ref[:, d + c * ch:d + (c + 1) * ch].astype(F32)
        m_ref[:, cols] = (g_pool * p + g_lru * q).astype(BF16)
    for c in range(d // ch):
        cols = slice(c * ch, (c + 1) * ch)
        o_ref[:, cols] = x_ref[:, cols] + _dot(m_ref[...], wout_ref[:, cols])


def _merge(y_pool, y_lru, gates, x, w_pool_proj, w_lru_proj, w_out):
    n, d = x.shape
    tm = MERGE_TM
    assert n % tm == 0 and d % MERGE_CHUNK == 0
    row = lambda i: (i, 0)
    return pl.pallas_call(
        _merge_kernel,
        grid=(n // tm,),
        in_specs=[
            pl.BlockSpec((tm, y_pool.shape[1]), row),
            pl.BlockSpec((tm, y_lru.shape[1]), row),
            pl.BlockSpec((tm, gates.shape[1]), row),
            pl.BlockSpec((tm, d), row),
            _resident(w_pool_proj.shape), _resident(w_lru_proj.shape), _resident(w_out.shape),
        ],
        out_specs=pl.BlockSpec((tm, d), row),
        out_shape=jax.ShapeDtypeStruct((n, d), F32),
        scratch_shapes=[pltpu.VMEM((tm, d), BF16)],
        compiler_params=pltpu.CompilerParams(
            dimension_semantics=("arbitrary",),
            vmem_limit_bytes=VMEM_LIMIT_BYTES),
    )(y_pool, y_lru, gates, x, w_pool_proj, w_lru_proj, w_out)


def _ffn_kernel(apply_final, x_ref, g_ref, wg_ref, wv_ref, cw_ref, cb_ref, wd_ref, gfin_ref,
                o_ref, h2_ref, ext_ref, carry_ref):
    i = pl.program_id(1)
    f = pl.program_id(2)
    t_rows = x_ref.shape[0]

    @pl.when(f == 0)
    def _():
        x = x_ref[...]
        h2_ref[...] = _rms_norm(x, g_ref[...]).astype(BF16)
        o_ref[...] = x

    @pl.when(i == 0)
    def _():
        ext_ref[0:CONV_HALO, :] = jnp.zeros((CONV_HALO, ext_ref.shape[1]), F32)

    @pl.when(i > 0)
    def _():
        ext_ref[0:CONV_HALO, :] = carry_ref[f]

    n_taps = cw_ref.shape[0]
    n_sub = t_rows // ROW_SUB
    ups = [(_dot(h2_ref[r * ROW_SUB:(r + 1) * ROW_SUB, :], wg_ref[...]),
            _dot(h2_ref[r * ROW_SUB:(r + 1) * ROW_SUB, :], wv_ref[...])) for r in range(n_sub)]
    for r in range(n_sub):
        rows = slice(r * ROW_SUB, (r + 1) * ROW_SUB)
        base = CONV_HALO + r * ROW_SUB
        gate_pre, val = ups[r]
        ext_ref[base:base + ROW_SUB, :] = gate_pre
        conv = cb_ref[...]
        for k in range(n_taps):
            conv = conv + (ext_ref[base - k:base - k + ROW_SUB, :]
                           * cw_ref[n_taps - 1 - k:n_taps - k, :])
        act = (jax.nn.gelu(conv) * val).astype(BF16)
        o_ref[rows, :] += _dot(act, wd_ref[...])

    carry_ref[f] = ext_ref[t_rows:t_rows + CONV_HALO, :]

    if apply_final:
        @pl.when(f == pl.num_programs(2) - 1)
        def _():
            o_ref[...] = _rms_norm(o_ref[...], gfin_ref[...])


def _ffn(x, g, w_up, conv_w, conv_b, w_down, g_final, apply_final, batch, seq):
    n, d = x.shape
    d_ff = w_down.shape[0]
    t, tf = FFN_T, FFN_TF
    assert seq % t == 0 and d_ff % tf == 0
    nblk, nf = seq // t, d_ff // tf
    row = lambda b, i, f: (b * nblk + i, 0)
    return pl.pallas_call(
        functools.partial(_ffn_kernel, apply_final),
        grid=(batch, nblk, nf),
        in_specs=[
            pl.BlockSpec((t, d), row),
            pl.BlockSpec((1, d), lambda b, i, f: (0, 0)),
            pl.BlockSpec((d, tf), lambda b, i, f: (0, f)),
            pl.BlockSpec((d, tf), lambda b, i, f: (0, nf + f)),
            pl.BlockSpec((conv_w.shape[0], tf), lambda b, i, f: (0, f)),
            pl.BlockSpec((1, tf), lambda b, i, f: (0, f)),
            pl.BlockSpec((tf, d), lambda b, i, f: (f, 0)),
            pl.BlockSpec((1, d), lambda b, i, f: (0, 0)),
        ],
        out_specs=pl.BlockSpec((t, d), row),
        out_shape=jax.ShapeDtypeStruct((n, d), F32),
        scratch_shapes=[
            pltpu.VMEM((t, d), BF16),
            pltpu.VMEM((CONV_HALO + t, tf), F32),
            pltpu.VMEM((nf, CONV_HALO, tf), F32),
        ],
        compiler_params=pltpu.CompilerParams(
            dimension_semantics=("arbitrary", "arbitrary", "arbitrary"),
            vmem_limit_bytes=VMEM_LIMIT_BYTES),
    )(x, g, w_up, w_up, conv_w, conv_b, w_down, g_final)


def kernel(x, g_mix, w_in, b_gate, w_pool, pool_scale, lru_conv_w, lru_conv_b, w_a, b_a, w_i,
           b_i, lru_lambda, w_pool_proj, w_lru_proj, w_out, g_mlp, w_up, ffn_conv_w, ffn_conv_b,
           w_down, g_final):
    batch, seq, d = x.shape
    depth = w_in.shape[0]
    pool_width = pool_scale.shape[1]
    lru_width = lru_lambda.shape[1]
    mix_width = pool_width + lru_width
    row2 = lambda v: v.reshape(1, -1)
    xf = x.reshape(batch * seq, d)
    for l in range(depth):
        d_pool, v, h, *w_gate_bf, wdn_bf = _mixproj(
            xf, row2(g_mix[l]), _cast_first_columns(w_in[l], mix_width), w_in[l], lru_conv_w[l],
            row2(lru_conv_b[l]), w_down[l], pool_width, seq)
        y_pool, a, bx, wpp_bf, wlp_bf, wout_bf = _mixer(
            d_pool, v, w_pool[l].astype(BF16), row2(pool_scale[l]),
            w_a[l].astype(BF16), row2(b_a[l]), w_i[l].astype(BF16), row2(b_i[l]),
            row2(lru_lambda[l]), (w_pool_proj[l], w_lru_proj[l], w_out[l]))
        y_lru, gates, wup_bf = _gateproj(h, a, bx, w_gate_bf, row2(b_gate[l]), w_up[l], seq)
        xf = _merge(y_pool, y_lru, gates, xf, wpp_bf, wlp_bf, wout_bf)
        xf = _ffn(xf, row2(g_mlp[l]), wup_bf, ffn_conv_w[l], row2(ffn_conv_b[l]),
                  wdn_bf, row2(g_final), l == depth - 1, batch, seq)
    return xf.reshape(batch, seq, d)
```

```python
import functools

import jax
import jax.numpy as jnp
from jax import lax
from jax.experimental import pallas as pl
from jax.experimental.pallas import tpu as pltpu

F32 = jnp.float32
BF16 = jnp.bfloat16

POOL_WINDOWS = (2, 4, 8, 16)
POOL_GROUP_WIDTH = 256
LRU_BLOCK_WIDTH = 256
LRU_C = 8.0
EPS = 1e-6

SUBLANES = 8
BF16_SUBLANES = 16
POOL_SUB = 4
POOL_SUB_HALO = 16
SEQ_HALO = 24
CONV_HALO = SUBLANES
VMEM_LIMIT_BYTES = 56 * 1024 * 1024

CAST_ROWS = 256
ROW_SUB = 256
PROJ_TN = 1024
MIXPROJ_TM = 512
GATEPROJ_TM = 256
GATEPROJ_GROUP = 2
MIXER_T = 512
MERGE_TM = 512
MERGE_CHUNK = 512
FFN_T = 512
FFN_TF = 1024


def _rms_norm(x, g):
    ms = jnp.mean(x * x, axis=-1, keepdims=True)
    return x * lax.rsqrt(ms + EPS) * g


def _dot(a, b):
    return jnp.dot(a, b, preferred_element_type=F32)


def _sigmoid(x):
    return 0.5 * jnp.tanh(0.5 * x) + 0.5


def _resident(shape):
    zeros = (0,) * len(shape)
    return pl.BlockSpec(shape, lambda *_: zeros, pipeline_mode=pl.Buffered(1))


def _slab_specs(weights, n_steps, step_index):
    specs = []
    for w in weights:
        rows, cols = w.shape
        assert rows % (n_steps * BF16_SUBLANES) == 0
        specs.append(pl.BlockSpec((rows // n_steps, cols), step_index))
    return specs


def _cast_slabs(src_refs, dst_refs):
    for src, dst in zip(src_refs, dst_refs):
        dst[...] = src[...].astype(BF16)


def _cast_kernel(src_ref, dst_ref):
    dst_ref[...] = src_ref[...].astype(BF16)


def _cast_first_columns(w, width):
    rows = w.shape[0]
    assert rows % CAST_ROWS == 0 and w.shape[1] % width == 0
    return pl.pallas_call(
        _cast_kernel,
        grid=(rows // CAST_ROWS,),
        in_specs=[pl.BlockSpec((CAST_ROWS, width), lambda i: (i, 0))],
        out_specs=pl.BlockSpec((CAST_ROWS, width), lambda i: (i, 0)),
        out_shape=jax.ShapeDtypeStruct((rows, width), BF16),
    )(w)


def _mixproj_kernel(blocks_per_seq, n_cast, x_ref, g_ref, w_ref, convw_ref, convb_ref, *refs):
    cast_refs = refs[:n_cast]
    d_ref, v_ref, hout_ref = refs[n_cast:n_cast + 3]
    cast_bf_refs = refs[n_cast + 3:2 * n_cast + 3]
    h_ref, ext_ref, sub_ref, carry_ref = refs[2 * n_cast + 3:]
    _cast_slabs(cast_refs, cast_bf_refs)
    tn = PROJ_TN
    n_sub = x_ref.shape[0] // ROW_SUB
    block_in_seq = lax.rem(pl.program_id(0), blocks_per_seq)

    @pl.when(block_in_seq == 0)
    def _():
        carry_ref[...] = jnp.zeros_like(carry_ref)

    @pl.loop(0, n_sub)
    def _(r):
        rows = pl.ds(pl.multiple_of(r * ROW_SUB, ROW_SUB), ROW_SUB)
        h = _rms_norm(x_ref[rows, :], g_ref[...]).astype(BF16)
        h_ref[rows, :] = h
        hout_ref[rows, :] = h

    def matmuls(tile):
        return [_dot(h_ref[r * ROW_SUB:(r + 1) * ROW_SUB, :], w_ref[:, tile * tn:(tile + 1) * tn])
                for r in range(n_sub)]

    def place(tile, r, p):
        slot = r % 2
        if r == 0:
            ext_ref[slot, 0:SEQ_HALO, :] = carry_ref[tile]
        ext_ref[slot, SEQ_HALO:SEQ_HALO + ROW_SUB, :] = p
        return slot

    def pass_tail(tile, r):
        tail = ext_ref[r % 2, ROW_SUB:ROW_SUB + SEQ_HALO, :]
        if r + 1 < n_sub:
            ext_ref[(r + 1) % 2, 0:SEQ_HALO, :] = tail
        else:
            carry_ref[tile] = tail

    def window(load, q0, n, count, stride):
        s = load(q0, n)
        for m in range(1, count):
            s = s + load(q0 - m * stride, n)
        return s

    wide = [g for g, w in enumerate(POOL_WINDOWS) if w > POOL_SUB]
    all_ps = [matmuls(tile) for tile in range(w_ref.shape[1] // tn)]
    ps = all_ps[0]
    for r in range(n_sub):
        rows = slice(r * ROW_SUB, (r + 1) * ROW_SUB)
        slot = place(0, r, ps[r])
        pos = (block_in_seq * x_ref.shape[0] + r * ROW_SUB + 1
               + lax.broadcasted_iota(jnp.int32, (ROW_SUB, 1), 0)).astype(F32)
        for g, w in enumerate(POOL_WINDOWS):
            cols = slice(g * POOL_GROUP_WIDTH, (g + 1) * POOL_GROUP_WIDTH)
            load_ext = lambda q0, n: ext_ref[slot, q0:q0 + n, cols]
            if w <= POOL_SUB:
                s = window(load_ext, SEQ_HALO, ROW_SUB, w, 1)
            else:
                scols = slice(wide.index(g) * POOL_GROUP_WIDTH,
                              (wide.index(g) + 1) * POOL_GROUP_WIDTH)
                n = POOL_SUB_HALO + ROW_SUB
                sub_ref[slot, 0:n, scols] = window(
                    load_ext, SEQ_HALO - POOL_SUB_HALO, n, POOL_SUB, 1)
                load_sub = lambda q0, n: sub_ref[slot, q0:q0 + n, scols]
                s = window(load_sub, POOL_SUB_HALO, ROW_SUB, w // POOL_SUB, POOL_SUB)
            inv_count = 1.0 / jnp.minimum(pos, float(w))
            d = s * inv_count - ext_ref[slot, SEQ_HALO:SEQ_HALO + ROW_SUB, cols]
            d_ref[rows, cols] = d.astype(BF16)
        pass_tail(0, r)

    n_taps = convw_ref.shape[0]
    for tile in range(1, w_ref.shape[1] // tn):
        cols = slice((tile - 1) * tn, tile * tn)
        ps = all_ps[tile]
        for r in range(n_sub):
            slot = place(tile, r, ps[r])
            v = convb_ref[:, cols]
            for k in range(n_taps):
                v = v + (ext_ref[slot, SEQ_HALO - k:SEQ_HALO - k + ROW_SUB, :]
                         * convw_ref[n_taps - 1 - k:n_taps - k, cols])
            v_ref[r * ROW_SUB:(r + 1) * ROW_SUB, cols] = v
            pass_tail(tile, r)


def _mixproj(x, g, w_mix, w_in, conv_w, conv_b, cast_weight, pool_width, seq):
    n, d = x.shape
    mix_width = w_mix.shape[1]
    lru_width = mix_width - pool_width
    tm, tn = MIXPROJ_TM, PROJ_TN
    assert pool_width == tn == len(POOL_WINDOWS) * POOL_GROUP_WIDTH and lru_width % tn == 0
    assert n % tm == 0 and seq % tm == 0 and tm % ROW_SUB == 0
    n_wide = sum(w > POOL_SUB for w in POOL_WINDOWS)
    n_steps = n // tm
    row = lambda i: (i, 0)
    assert w_in.shape[1] % mix_width == 0 and d % (n_steps * BF16_SUBLANES) == 0
    n_gblocks = w_in.shape[1] // mix_width - 1
    gate_in_specs = [pl.BlockSpec((d // n_steps, mix_width),
                                  functools.partial(lambda k, i: (i, k), 1 + k))
                     for k in range(n_gblocks)]
    gate_out_specs = [pl.BlockSpec((d // n_steps, mix_width), row)] * n_gblocks
    return pl.pallas_call(
        functools.partial(_mixproj_kernel, seq // tm, n_gblocks + 1),
        grid=(n_steps,),
        in_specs=[
            pl.BlockSpec((tm, d), row),
            _resident((1, d)), _resident(w_mix.shape),
            _resident(conv_w.shape), _resident(conv_b.shape),
            *gate_in_specs, *_slab_specs((cast_weight,), n_steps, row),
        ],
        out_specs=[pl.BlockSpec((tm, pool_width), row), pl.BlockSpec((tm, lru_width), row),
                   pl.BlockSpec((tm, d), row),
                   *gate_out_specs, *_slab_specs((cast_weight,), n_steps, row)],
        out_shape=[
            jax.ShapeDtypeStruct((n, pool_width), BF16),
            jax.ShapeDtypeStruct((n, lru_width), F32),
            jax.ShapeDtypeStruct((n, d), BF16),
            *[jax.ShapeDtypeStruct((d, mix_width), BF16)] * n_gblocks,
            jax.ShapeDtypeStruct(cast_weight.shape, BF16),
        ],
        scratch_shapes=[
            pltpu.VMEM((tm, d), BF16),
            pltpu.VMEM((2, SEQ_HALO + ROW_SUB, tn), F32),
            pltpu.VMEM((2, POOL_SUB_HALO + ROW_SUB, n_wide * POOL_GROUP_WIDTH), F32),
            pltpu.VMEM((mix_width // tn, SEQ_HALO, tn), F32),
        ],
        compiler_params=pltpu.CompilerParams(
            dimension_semantics=("arbitrary",),
            vmem_limit_bytes=VMEM_LIMIT_BYTES),
    )(x, g, w_mix, conv_w, conv_b, *([w_in] * n_gblocks), cast_weight)


def _gateproj_kernel(n_gelu, n_wblocks, n_cast, blocks_per_seq, h_ref, a_ref, bx_ref, *refs):
    w_refs = refs[:n_wblocks]
    b_ref = refs[n_wblocks]
    cast_refs = refs[n_wblocks + 1:n_wblocks + 1 + n_cast]
    ylru_ref, gates_ref = refs[n_wblocks + 1 + n_cast:n_wblocks + 3 + n_cast]
    cast_bf_refs = refs[n_wblocks + 3 + n_cast:n_wblocks + 3 + 2 * n_cast]
    hs_ref, hcarry_ref = refs[n_wblocks + 3 + 2 * n_cast:]
    _cast_slabs(cast_refs, cast_bf_refs)
    t_rows = h_ref.shape[0]
    tn = PROJ_TN
    tiles_per_block = w_refs[0].shape[1] // tn
    n_tiles = n_wblocks * tiles_per_block

    @pl.when(lax.rem(pl.program_id(0), blocks_per_seq) == 0)
    def _():
        hcarry_ref[...] = jnp.zeros_like(hcarry_ref)

    h = hcarry_ref[...]
    for t in range(t_rows):
        h = a_ref[t:t + 1, :] * h + bx_ref[t:t + 1, :]
        hs_ref[t:t + 1, :] = h
    hcarry_ref[...] = h

    def project(tile):
        w_ref = w_refs[tile // tiles_per_block]
        wcols = slice((tile % tiles_per_block) * tn, (tile % tiles_per_block + 1) * tn)
        return _dot(h_ref[...], w_ref[:, wcols])

    def finish(tile, p):
        if tile < n_gelu:
            cols = slice(tile * tn, (tile + 1) * tn)
            ylru_ref[:, cols] = (hs_ref[:, cols] * jax.nn.gelu(p)).astype(BF16)
        else:
            cols = slice((tile - n_gelu) * tn, (tile - n_gelu + 1) * tn)
            gates_ref[:, cols] = jax.nn.sigmoid(p + b_ref[:, cols]).astype(BF16)

    for t0 in range(0, n_tiles, GATEPROJ_GROUP):
        group = range(t0, min(t0 + GATEPROJ_GROUP, n_tiles))
        ps = [project(tile) for tile in group]
        for tile, p in zip(group, ps):
            finish(tile, p)


def _gateproj(h, a, bx, w_blocks, b_gate, cast_weights, seq):
    n, d = h.shape
    gelu_width = a.shape[1]
    n_wblocks = len(w_blocks)
    gate_width = sum(w.shape[1] for w in w_blocks) - gelu_width
    tm, tn = GATEPROJ_TM, PROJ_TN
    assert n % tm == 0 and seq % tm == 0 and gelu_width % tn == 0 and gate_width % tn == 0
    assert all(w.shape == w_blocks[0].shape for w in w_blocks) and w_blocks[0].shape[1] % tn == 0
    row = lambda i: (i, 0)
    w_specs = [_resident(w.shape) for w in w_blocks]
    return pl.pallas_call(
        functools.partial(_gateproj_kernel, gelu_width // tn, n_wblocks, len(cast_weights),
                          seq // tm),
        grid=(n // tm,),
        in_specs=[pl.BlockSpec((tm, d), row), pl.BlockSpec((tm, gelu_width), row),
                  pl.BlockSpec((tm, gelu_width), row), *w_specs, _resident(b_gate.shape),
                  *_slab_specs(cast_weights, n // tm, row)],
        out_specs=[pl.BlockSpec((tm, gelu_width), row), pl.BlockSpec((tm, gate_width), row),
                   *_slab_specs(cast_weights, n // tm, row)],
        out_shape=[jax.ShapeDtypeStruct((n, gelu_width), BF16),
                   jax.ShapeDtypeStruct((n, gate_width), BF16),
                   *[jax.ShapeDtypeStruct(w.shape, BF16) for w in cast_weights]],
        scratch_shapes=[
            pltpu.VMEM((tm, gelu_width), F32),
            pltpu.VMEM((1, gelu_width), F32),
        ],
        compiler_params=pltpu.CompilerParams(
            dimension_semantics=("arbitrary",),
            vmem_limit_bytes=VMEM_LIMIT_BYTES),
    )(h, a, bx, *w_blocks, b_gate, *cast_weights)


def _mixer_kernel(d_ref, v_ref, wpool_ref, pscale_ref, wa_ref, ba_ref, wi_ref, bi_ref, lam_ref,
                  ypool_ref, a_ref, bx_ref):
    lru_width = a_ref.shape[1]

    for g in range(len(POOL_WINDOWS)):
        cols = slice(g * POOL_GROUP_WIDTH, (g + 1) * POOL_GROUP_WIDTH)
        y = _dot(d_ref[:, cols], wpool_ref[g]) * pscale_ref[:, cols]
        ypool_ref[:, cols] = y.astype(BF16)

    half_log_a_scale = (-0.5 * LRU_C) * jax.nn.softplus(-lam_ref[...])
    for hd in range(lru_width // LRU_BLOCK_WIDTH):
        cols = slice(hd * LRU_BLOCK_WIDTH, (hd + 1) * LRU_BLOCK_WIDTH)
        v = v_ref[:, cols]
        vb = v.astype(BF16)
        x_r = _dot(vb, wa_ref[hd]) + ba_ref[:, cols]
        ig = _sigmoid(_dot(vb, wi_ref[hd]) + bi_ref[:, cols])
        log_a = (jnp.tanh(0.5 * x_r) + 1.0) * half_log_a_scale[:, cols]
        a = jnp.exp(log_a)
        q = -jnp.tanh(log_a) * (a * a + 1.0)
        mult = jnp.where(q > 0.0, q * lax.rsqrt(q), 0.0)
        a_ref[:, cols] = a
        bx_ref[:, cols] = mult * (ig * v)


def _mixer(d_pool, v, w_pool, pool_scale, w_a, b_a, w_i, b_i, lam):
    n, pool_width = d_pool.shape
    lru_width = v.shape[1]
    t = MIXER_T
    assert n % t == 0
    row = lambda i: (i, 0)
    return pl.pallas_call(
        _mixer_kernel,
        grid=(n // t,),
        in_specs=[
            pl.BlockSpec((t, pool_width), row),
            pl.BlockSpec((t, lru_width), row),
            _resident(w_pool.shape), _resident(pool_scale.shape),
            _resident(w_a.shape), _resident(b_a.shape),
            _resident(w_i.shape), _resident(b_i.shape), _resident(lam.shape),
        ],
        out_specs=[pl.BlockSpec((t, pool_width), row), pl.BlockSpec((t, lru_width), row),
                   pl.BlockSpec((t, lru_width), row)],
        out_shape=[jax.ShapeDtypeStruct((n, pool_width), BF16),
                   jax.ShapeDtypeStruct((n, lru_width), F32),
                   jax.ShapeDtypeStruct((n, lru_width), F32)],
        compiler_params=pltpu.CompilerParams(
            dimension_semantics=("arbitrary",),
            vmem_limit_bytes=VMEM_LIMIT_BYTES),
    )(d_pool, v, w_pool, pool_scale, w_a, b_a, w_i, b_i, lam)


def _merge_kernel(yp_ref, yl_ref, gates_ref, x_ref, wpp_ref, wlp_ref, wout_ref, o_ref, m_ref):
    d = o_ref.shape[1]
    ch = MERGE_CHUNK
    for c in range(d // ch):
        cols = slice(c * ch, (c + 1) * ch)
        p = _dot(yp_ref[...], wpp_ref[:, cols])
        q = _dot(yl_ref[...], wlp_ref[:, cols])
        g_pool = gates_ref[:, cols].astype(F32)
        g_lru = gates_ref[:, d + c * ch:d + (c + 1) * ch].astype(F32)
        m_ref[:, cols] = (g_pool * p + g_lru * q).astype(BF16)
    for c in range(d // ch):
        cols = slice(c * ch, (c + 1) * ch)
        o_ref[:, cols] = x_ref[:, cols] + _dot(m_ref[...], wout_ref[:, cols])


def _merge(y_pool, y_lru, gates, x, w_pool_proj, w_lru_proj, w_out):
    n, d = x.shape
    tm = MERGE_TM
    assert n % tm == 0 and d % MERGE_CHUNK == 0
    row = lambda i: (i, 0)
    return pl.pallas_call(
        _merge_kernel,
        grid=(n // tm,),
        in_specs=[
            pl.BlockSpec((tm, y_pool.shape[1]), row),
            pl.BlockSpec((tm, y_lru.shape[1]), row),
            pl.BlockSpec((tm, gates.shape[1]), row),
            pl.BlockSpec((tm, d), row),
            _resident(w_pool_proj.shape), _resident(w_lru_proj.shape), _resident(w_out.shape),
        ],
        out_specs=pl.BlockSpec((tm, d), row),
        out_shape=jax.ShapeDtypeStruct((n, d), F32),
        scratch_shapes=[pltpu.VMEM((tm, d), BF16)],
        compiler_params=pltpu.CompilerParams(
            dimension_semantics=("arbitrary",),
            vmem_limit_bytes=VMEM_LIMIT_BYTES),
    )(y_pool, y_lru, gates, x, w_pool_proj, w_lru_proj, w_out)


def _ffn_kernel(apply_final, x_ref, g_ref, wg_ref, wv_ref, cw_ref, cb_ref, wd_ref, gfin_ref,
                o_ref, h2_ref, ext_ref, carry_ref):
    i = pl.program_id(1)
    f = pl.program_id(2)
    t_rows = x_ref.shape[0]

    @pl.when(f == 0)
    def _():
        x = x_ref[...]
        h2_ref[...] = _rms_norm(x, g_ref[...]).astype(BF16)
        o_ref[...] = x

    @pl.when(i == 0)
    def _():
        ext_ref[0:CONV_HALO, :] = jnp.zeros((CONV_HALO, ext_ref.shape[1]), F32)

    @pl.when(i > 0)
    def _():
        ext_ref[0:CONV_HALO, :] = carry_ref[f]

    n_taps = cw_ref.shape[0]
    n_sub = t_rows // ROW_SUB
    ups = [(_dot(h2_ref[r * ROW_SUB:(r + 1) * ROW_SUB, :], wg_ref[...]),
            _dot(h2_ref[r * ROW_SUB:(r + 1) * ROW_SUB, :], wv_ref[...])) for r in range(n_sub)]
    for r in range(n_sub):
        rows = slice(r * ROW_SUB, (r + 1) * ROW_SUB)
        base = CONV_HALO + r * ROW_SUB
        gate_pre, val = ups[r]
        ext_ref[base:base + ROW_SUB, :] = gate_pre
        conv = cb_ref[...]
        for k in range(n_taps):
            conv = conv + (ext_ref[base - k:base - k + ROW_SUB, :]
                           * cw_ref[n_taps - 1 - k:n_taps - k, :])
        act = (jax.nn.gelu(conv) * val).astype(BF16)
        o_ref[rows, :] += _dot(act, wd_ref[...])

    carry_ref[f] = ext_ref[t_rows:t_rows + CONV_HALO, :]

    if apply_final:
        @pl.when(f == pl.num_programs(2) - 1)
        def _():
            o_ref[...] = _rms_norm(o_ref[...], gfin_ref[...])


def _ffn(x, g, w_up, conv_w, conv_b, w_down, g_final, apply_final, batch, seq):
    n, d = x.shape
    d_ff = w_down.shape[0]
    t, tf = FFN_T, FFN_TF
    assert seq % t == 0 and d_ff % tf == 0
    nblk, nf = seq // t, d_ff // tf
    row = lambda b, i, f: (b * nblk + i, 0)
    return pl.pallas_call(
        functools.partial(_ffn_kernel, apply_final),
        grid=(batch, nblk, nf),
        in_specs=[
            pl.BlockSpec((t, d), row),
            pl.BlockSpec((1, d), lambda b, i, f: (0, 0)),
            pl.BlockSpec((d, tf), lambda b, i, f: (0, f)),
            pl.BlockSpec((d, tf), lambda b, i, f: (0, nf + f)),
            pl.BlockSpec((conv_w.shape[0], tf), lambda b, i, f: (0, f)),
            pl.BlockSpec((1, tf), lambda b, i, f: (0, f)),
            pl.BlockSpec((tf, d), lambda b, i, f: (f, 0)),
            pl.BlockSpec((1, d), lambda b, i, f: (0, 0)),
        ],
        out_specs=pl.BlockSpec((t, d), row),
        out_shape=jax.ShapeDtypeStruct((n, d), F32),
        scratch_shapes=[
            pltpu.VMEM((t, d), BF16),
            pltpu.VMEM((CONV_HALO + t, tf), F32),
            pltpu.VMEM((nf, CONV_HALO, tf), F32),
        ],
        compiler_params=pltpu.CompilerParams(
            dimension_semantics=("arbitrary", "arbitrary", "arbitrary"),
            vmem_limit_bytes=VMEM_LIMIT_BYTES),
    )(x, g, w_up, w_up, conv_w, conv_b, w_down, g_final)


def kernel(x, g_mix, w_in, b_gate, w_pool, pool_scale, lru_conv_w, lru_conv_b, w_a, b_a, w_i,
           b_i, lru_lambda, w_pool_proj, w_lru_proj, w_out, g_mlp, w_up, ffn_conv_w, ffn_conv_b,
           w_down, g_final):
    batch, seq, d = x.shape
    depth = w_in.shape[0]
    pool_width = pool_scale.shape[1]
    lru_width = lru_lambda.shape[1]
    mix_width = pool_width + lru_width
    row2 = lambda v: v.reshape(1, -1)
    xf = x.reshape(batch * seq, d)
    for l in range(depth):
        d_pool, v, h, *w_gate_bf, wdn_bf = _mixproj(
            xf, row2(g_mix[l]), _cast_first_columns(w_in[l], mix_width), w_in[l], lru_conv_w[l],
            row2(lru_conv_b[l]), w_down[l], pool_width, seq)
        y_pool, a, bx = _mixer(
            d_pool, v, w_pool[l].astype(BF16), row2(pool_scale[l]),
            w_a[l].astype(BF16), row2(b_a[l]), w_i[l].astype(BF16), row2(b_i[l]),
            row2(lru_lambda[l]))
        y_lru, gates, wup_bf, wpp_bf, wlp_bf, wout_bf = _gateproj(
            h, a, bx, w_gate_bf, row2(b_gate[l]),
            (w_up[l], w_pool_proj[l], w_lru_proj[l], w_out[l]), seq)
        xf = _merge(y_pool, y_lru, gates, xf, wpp_bf, wlp_bf, wout_bf)
        xf = _ffn(xf, row2(g_mlp[l]), wup_bf, ffn_conv_w[l], row2(ffn_conv_b[l]),
                  wdn_bf, row2(g_final), l == depth - 1, batch, seq)
    return xf.reshape(batch, seq, d)
```

```python
import functools

import jax
import jax.numpy as jnp
from jax import lax
from jax.experimental import pallas as pl
from jax.experimental.pallas import tpu as pltpu

F32 = jnp.float32
BF16 = jnp.bfloat16

POOL_WINDOWS = (2, 4, 8, 16)
POOL_GROUP_WIDTH = 256
LRU_BLOCK_WIDTH = 256
LRU_C = 8.0
EPS = 1e-6

SUBLANES = 8
BF16_SUBLANES = 16
POOL_SUB = 4
POOL_SUB_HALO = 16
SEQ_HALO = 24
CONV_HALO = SUBLANES
VMEM_LIMIT_BYTES = 56 * 1024 * 1024

CAST_ROWS = 256
ROW_SUB = 256
PROJ_TN = 1024
MIXPROJ_TM = 512
GATEPROJ_TM = 256
GATEPROJ_GROUP = 2
MIXER_T = 512
MIXER_RING = 3
MERGE_TM = 512
MERGE_CHUNK = 512
FFN_T = 512
FFN_TF = 1024


def _rms_norm(x, g):
    ms = jnp.mean(x * x, axis=-1, keepdims=True)
    return x * lax.rsqrt(ms + EPS) * g


def _dot(a, b):
    return jnp.dot(a, b, preferred_element_type=F32)


def _sigmoid(x):
    return 0.5 * jnp.tanh(0.5 * x) + 0.5


def _resident(shape):
    zeros = (0,) * len(shape)
    return pl.BlockSpec(shape, lambda *_: zeros, pipeline_mode=pl.Buffered(1))


def _slab_specs(weights, n_steps, step_index):
    specs = []
    for w in weights:
        rows, cols = w.shape
        assert rows % (n_steps * BF16_SUBLANES) == 0
        specs.append(pl.BlockSpec((rows // n_steps, cols), step_index))
    return specs


def _cast_slabs(src_refs, dst_refs):
    for src, dst in zip(src_refs, dst_refs):
        dst[...] = src[...].astype(BF16)


def _cast_kernel(src_ref, dst_ref):
    dst_ref[...] = src_ref[...].astype(BF16)


def _cast_first_columns(w, width):
    rows = w.shape[0]
    assert rows % CAST_ROWS == 0 and w.shape[1] % width == 0
    return pl.pallas_call(
        _cast_kernel,
        grid=(rows // CAST_ROWS,),
        in_specs=[pl.BlockSpec((CAST_ROWS, width), lambda i: (i, 0))],
        out_specs=pl.BlockSpec((CAST_ROWS, width), lambda i: (i, 0)),
        out_shape=jax.ShapeDtypeStruct((rows, width), BF16),
    )(w)


def _mixproj_kernel(blocks_per_seq, n_cast, x_ref, g_ref, w_ref, convw_ref, convb_ref, *refs):
    cast_refs = refs[:n_cast]
    d_ref, v_ref, hout_ref = refs[n_cast:n_cast + 3]
    cast_bf_refs = refs[n_cast + 3:2 * n_cast + 3]
    h_ref, ext_ref, sub_ref, carry_ref = refs[2 * n_cast + 3:]
    _cast_slabs(cast_refs, cast_bf_refs)
    tn = PROJ_TN
    n_sub = x_ref.shape[0] // ROW_SUB
    block_in_seq = lax.rem(pl.program_id(0), blocks_per_seq)

    @pl.when(block_in_seq == 0)
    def _():
        carry_ref[...] = jnp.zeros_like(carry_ref)

    @pl.loop(0, n_sub)
    def _(r):
        rows = pl.ds(pl.multiple_of(r * ROW_SUB, ROW_SUB), ROW_SUB)
        h = _rms_norm(x_ref[rows, :], g_ref[...]).astype(BF16)
        h_ref[rows, :] = h
        hout_ref[rows, :] = h

    def matmuls(tile):
        return [_dot(h_ref[r * ROW_SUB:(r + 1) * ROW_SUB, :], w_ref[:, tile * tn:(tile + 1) * tn])
                for r in range(n_sub)]

    def place(tile, r, p):
        slot = r % 2
        if r == 0:
            ext_ref[slot, 0:SEQ_HALO, :] = carry_ref[tile]
        ext_ref[slot, SEQ_HALO:SEQ_HALO + ROW_SUB, :] = p
        return slot

    def pass_tail(tile, r):
        tail = ext_ref[r % 2, ROW_SUB:ROW_SUB + SEQ_HALO, :]
        if r + 1 < n_sub:
            ext_ref[(r + 1) % 2, 0:SEQ_HALO, :] = tail
        else:
            carry_ref[tile] = tail

    def window(load, q0, n, count, stride):
        s = load(q0, n)
        for m in range(1, count):
            s = s + load(q0 - m * stride, n)
        return s

    wide = [g for g, w in enumerate(POOL_WINDOWS) if w > POOL_SUB]
    all_ps = [matmuls(tile) for tile in range(w_ref.shape[1] // tn)]
    ps = all_ps[0]
    for r in range(n_sub):
        rows = slice(r * ROW_SUB, (r + 1) * ROW_SUB)
        slot = place(0, r, ps[r])
        pos = (block_in_seq * x_ref.shape[0] + r * ROW_SUB + 1
               + lax.broadcasted_iota(jnp.int32, (ROW_SUB, 1), 0)).astype(F32)
        for g, w in enumerate(POOL_WINDOWS):
            cols = slice(g * POOL_GROUP_WIDTH, (g + 1) * POOL_GROUP_WIDTH)
            load_ext = lambda q0, n: ext_ref[slot, q0:q0 + n, cols]
            if w <= POOL_SUB:
                s = window(load_ext, SEQ_HALO, ROW_SUB, w, 1)
            else:
                scols = slice(wide.index(g) * POOL_GROUP_WIDTH,
                              (wide.index(g) + 1) * POOL_GROUP_WIDTH)
                n = POOL_SUB_HALO + ROW_SUB
                sub_ref[slot, 0:n, scols] = window(
                    load_ext, SEQ_HALO - POOL_SUB_HALO, n, POOL_SUB, 1)
                load_sub = lambda q0, n: sub_ref[slot, q0:q0 + n, scols]
                s = window(load_sub, POOL_SUB_HALO, ROW_SUB, w // POOL_SUB, POOL_SUB)
            inv_count = 1.0 / jnp.minimum(pos, float(w))
            d = s * inv_count - ext_ref[slot, SEQ_HALO:SEQ_HALO + ROW_SUB, cols]
            d_ref[rows, cols] = d.astype(BF16)
        pass_tail(0, r)

    n_taps = convw_ref.shape[0]
    for tile in range(1, w_ref.shape[1] // tn):
        cols = slice((tile - 1) * tn, tile * tn)
        ps = all_ps[tile]
        for r in range(n_sub):
            slot = place(tile, r, ps[r])
            v = convb_ref[:, cols]
            for k in range(n_taps):
                v = v + (ext_ref[slot, SEQ_HALO - k:SEQ_HALO - k + ROW_SUB, :]
                         * convw_ref[n_taps - 1 - k:n_taps - k, cols])
            v_ref[r * ROW_SUB:(r + 1) * ROW_SUB, cols] = v
            pass_tail(tile, r)


def _mixproj(x, g, w_mix, w_in, conv_w, conv_b, cast_weight, pool_width, seq):
    n, d = x.shape
    mix_width = w_mix.shape[1]
    lru_width = mix_width - pool_width
    tm, tn = MIXPROJ_TM, PROJ_TN
    assert pool_width == tn == len(POOL_WINDOWS) * POOL_GROUP_WIDTH and lru_width % tn == 0
    assert n % tm == 0 and seq % tm == 0 and tm % ROW_SUB == 0
    n_wide = sum(w > POOL_SUB for w in POOL_WINDOWS)
    n_steps = n // tm
    row = lambda i: (i, 0)
    assert w_in.shape[1] % mix_width == 0 and d % (n_steps * BF16_SUBLANES) == 0
    n_gblocks = w_in.shape[1] // mix_width - 1
    gate_in_specs = [pl.BlockSpec((d // n_steps, mix_width),
                                  functools.partial(lambda k, i: (i, k), 1 + k))
                     for k in range(n_gblocks)]
    gate_out_specs = [pl.BlockSpec((d // n_steps, mix_width), row)] * n_gblocks
    return pl.pallas_call(
        functools.partial(_mixproj_kernel, seq // tm, n_gblocks + 1),
        grid=(n_steps,),
        in_specs=[
            pl.BlockSpec((tm, d), row),
            _resident((1, d)), _resident(w_mix.shape),
            _resident(conv_w.shape), _resident(conv_b.shape),
            *gate_in_specs, *_slab_specs((cast_weight,), n_steps, row),
        ],
        out_specs=[pl.BlockSpec((tm, pool_width), row), pl.BlockSpec((tm, lru_width), row),
                   pl.BlockSpec((tm, d), row),
                   *gate_out_specs, *_slab_specs((cast_weight,), n_steps, row)],
        out_shape=[
            jax.ShapeDtypeStruct((n, pool_width), BF16),
            jax.ShapeDtypeStruct((n, lru_width), F32),
            jax.ShapeDtypeStruct((n, d), BF16),
            *[jax.ShapeDtypeStruct((d, mix_width), BF16)] * n_gblocks,
            jax.ShapeDtypeStruct(cast_weight.shape, BF16),
        ],
        scratch_shapes=[
            pltpu.VMEM((tm, d), BF16),
            pltpu.VMEM((2, SEQ_HALO + ROW_SUB, tn), F32),
            pltpu.VMEM((2, POOL_SUB_HALO + ROW_SUB, n_wide * POOL_GROUP_WIDTH), F32),
            pltpu.VMEM((mix_width // tn, SEQ_HALO, tn), F32),
        ],
        compiler_params=pltpu.CompilerParams(
            dimension_semantics=("arbitrary",),
            vmem_limit_bytes=VMEM_LIMIT_BYTES),
    )(x, g, w_mix, conv_w, conv_b, *([w_in] * n_gblocks), cast_weight)


def _gateproj_kernel(n_gelu, n_wblocks, n_cast, blocks_per_seq, h_ref, a_ref, bx_ref, *refs):
    w_refs = refs[:n_wblocks]
    b_ref = refs[n_wblocks]
    cast_refs = refs[n_wblocks + 1:n_wblocks + 1 + n_cast]
    ylru_ref, gates_ref = refs[n_wblocks + 1 + n_cast:n_wblocks + 3 + n_cast]
    cast_bf_refs = refs[n_wblocks + 3 + n_cast:n_wblocks + 3 + 2 * n_cast]
    hs_ref, hcarry_ref = refs[n_wblocks + 3 + 2 * n_cast:]
    _cast_slabs(cast_refs, cast_bf_refs)
    t_rows = h_ref.shape[0]
    tn = PROJ_TN
    tiles_per_block = w_refs[0].shape[1] // tn
    n_tiles = n_wblocks * tiles_per_block

    @pl.when(lax.rem(pl.program_id(0), blocks_per_seq) == 0)
    def _():
        hcarry_ref[...] = jnp.zeros_like(hcarry_ref)

    h = hcarry_ref[...]
    for t in range(t_rows):
        h = a_ref[t:t + 1, :] * h + bx_ref[t:t + 1, :]
        hs_ref[t:t + 1, :] = h
    hcarry_ref[...] = h

    def project(tile):
        w_ref = w_refs[tile // tiles_per_block]
        wcols = slice((tile % tiles_per_block) * tn, (tile % tiles_per_block + 1) * tn)
        return _dot(h_ref[...], w_ref[:, wcols])

    def finish(tile, p):
        if tile < n_gelu:
            cols = slice(tile * tn, (tile + 1) * tn)
            ylru_ref[:, cols] = (hs_ref[:, cols] * jax.nn.gelu(p)).astype(BF16)
        else:
            cols = slice((tile - n_gelu) * tn, (tile - n_gelu + 1) * tn)
            gates_ref[:, cols] = jax.nn.sigmoid(p + b_ref[:, cols]).astype(BF16)

    for t0 in range(0, n_tiles, GATEPROJ_GROUP):
        group = range(t0, min(t0 + GATEPROJ_GROUP, n_tiles))
        ps = [project(tile) for tile in group]
        for tile, p in zip(group, ps):
            finish(tile, p)


def _gateproj(h, a, bx, w_blocks, b_gate, cast_weights, seq):
    n, d = h.shape
    gelu_width = a.shape[1]
    n_wblocks = len(w_blocks)
    gate_width = sum(w.shape[1] for w in w_blocks) - gelu_width
    tm, tn = GATEPROJ_TM, PROJ_TN
    assert n % tm == 0 and seq % tm == 0 and gelu_width % tn == 0 and gate_width % tn == 0
    assert all(w.shape == w_blocks[0].shape for w in w_blocks) and w_blocks[0].shape[1] % tn == 0
    row = lambda i: (i, 0)
    w_specs = [_resident(w.shape) for w in w_blocks]
    return pl.pallas_call(
        functools.partial(_gateproj_kernel, gelu_width // tn, n_wblocks, len(cast_weights),
                          seq // tm),
        grid=(n // tm,),
        in_specs=[pl.BlockSpec((tm, d), row), pl.BlockSpec((tm, gelu_width), row),
                  pl.BlockSpec((tm, gelu_width), row), *w_specs, _resident(b_gate.shape),
                  *_slab_specs(cast_weights, n // tm, row)],
        out_specs=[pl.BlockSpec((tm, gelu_width), row), pl.BlockSpec((tm, gate_width), row),
                   *_slab_specs(cast_weights, n // tm, row)],
        out_shape=[jax.ShapeDtypeStruct((n, gelu_width), BF16),
                   jax.ShapeDtypeStruct((n, gate_width), BF16),
                   *[jax.ShapeDtypeStruct(w.shape, BF16) for w in cast_weights]],
        scratch_shapes=[
            pltpu.VMEM((tm, gelu_width), F32),
            pltpu.VMEM((1, gelu_width), F32),
        ],
        compiler_params=pltpu.CompilerParams(
            dimension_semantics=("arbitrary",),
            vmem_limit_bytes=VMEM_LIMIT_BYTES),
    )(h, a, bx, *w_blocks, b_gate, *cast_weights)


def _mixer_kernel(d_ref, v_hbm_ref, wpool_ref, pscale_ref, wa_ref, ba_ref, wi_ref, bi_ref, lam_ref,
                  ypool_ref, a_ref, bx_ref, vbuf_ref, vsem_ref):
    t_rows, lru_width = a_ref.shape
    s = pl.program_id(0)
    n_steps = pl.num_programs(0)

    def v_copy(step):
        slot = lax.rem(step, MIXER_RING)
        rows = pl.ds(pl.multiple_of(step * t_rows, t_rows), t_rows)
        return pltpu.make_async_copy(v_hbm_ref.at[rows, :], vbuf_ref.at[slot], vsem_ref.at[slot])

    @pl.when(s == 0)
    def _():
        for step in range(MIXER_RING - 1):
            v_copy(step).start()

    @pl.when(s + (MIXER_RING - 1) < n_steps)
    def _():
        v_copy(s + (MIXER_RING - 1)).start()

    v_copy(s).wait()
    v_ref = vbuf_ref.at[lax.rem(s, MIXER_RING)]

    for g in range(len(POOL_WINDOWS)):
        cols = slice(g * POOL_GROUP_WIDTH, (g + 1) * POOL_GROUP_WIDTH)
        y = _dot(d_ref[:, cols], wpool_ref[g]) * pscale_ref[:, cols]
        ypool_ref[:, cols] = y.astype(BF16)

    half_log_a_scale = (-0.5 * LRU_C) * jax.nn.softplus(-lam_ref[...])
    for hd in range(lru_width // LRU_BLOCK_WIDTH):
        cols = slice(hd * LRU_BLOCK_WIDTH, (hd + 1) * LRU_BLOCK_WIDTH)
        v = v_ref[:, cols]
        vb = v.astype(BF16)
        x_r = _dot(vb, wa_ref[hd]) + ba_ref[:, cols]
        ig = _sigmoid(_dot(vb, wi_ref[hd]) + bi_ref[:, cols])
        log_a = (jnp.tanh(0.5 * x_r) + 1.0) * half_log_a_scale[:, cols]
        a = jnp.exp(log_a)
        q = -jnp.tanh(log_a) * (a * a + 1.0)
        mult = jnp.where(q > 0.0, q * lax.rsqrt(q), 0.0)
        a_ref[:, cols] = a
        bx_ref[:, cols] = mult * (ig * v)


def _mixer(d_pool, v, w_pool, pool_scale, w_a, b_a, w_i, b_i, lam):
    n, pool_width = d_pool.shape
    lru_width = v.shape[1]
    t = MIXER_T
    assert n % t == 0 and n // t >= MIXER_RING - 1
    row = lambda i: (i, 0)
    return pl.pallas_call(
        _mixer_kernel,
        grid=(n // t,),
        in_specs=[
            pl.BlockSpec((t, pool_width), row),
            pl.BlockSpec(memory_space=pl.ANY),
            _resident(w_pool.shape), _resident(pool_scale.shape),
            _resident(w_a.shape), _resident(b_a.shape),
            _resident(w_i.shape), _resident(b_i.shape), _resident(lam.shape),
        ],
        out_specs=[pl.BlockSpec((t, pool_width), row), pl.BlockSpec((t, lru_width), row),
                   pl.BlockSpec((t, lru_width), row)],
        out_shape=[jax.ShapeDtypeStruct((n, pool_width), BF16),
                   jax.ShapeDtypeStruct((n, lru_width), F32),
                   jax.ShapeDtypeStruct((n, lru_width), F32)],
        scratch_shapes=[pltpu.VMEM((MIXER_RING, t, lru_width), F32),
                        pltpu.SemaphoreType.DMA((MIXER_RING,))],
        compiler_params=pltpu.CompilerParams(
            dimension_semantics=("arbitrary",),
            vmem_limit_bytes=VMEM_LIMIT_BYTES),
    )(d_pool, v, w_pool, pool_scale, w_a, b_a, w_i, b_i, lam)


def _merge_kernel(yp_ref, yl_ref, gates_ref, x_ref, wpp_ref, wlp_ref, wout_ref, o_ref, m_ref):
    d = o_ref.shape[1]
    ch = MERGE_CHUNK
    for c in range(d // ch):
        cols = slice(c * ch, (c + 1) * ch)
        p = _dot(yp_ref[...], wpp_ref[:, cols])
        q = _dot(yl_ref[...], wlp_ref[:, cols])
        g_pool = gates_ref[:, cols].astype(F32)
        g_lru = gates_ref[:, d + c * ch:d + (c + 1) * ch].astype(F32)
        m_ref[:, cols] = (g_pool * p + g_lru * q).astype(BF16)
    for c in range(d // ch):
        cols = slice(c * ch, (c + 1) * ch)
        o_ref[:, cols] = x_ref[:, cols] + _dot(m_ref[...], wout_ref[:, cols])


def _merge(y_pool, y_lru, gates, x, w_pool_proj, w_lru_proj, w_out):
    n, d = x.shape
    tm = MERGE_TM
    assert n % tm == 0 and d % MERGE_CHUNK == 0
    row = lambda i: (i, 0)
    return pl.pallas_call(
        _merge_kernel,
        grid=(n // tm,),
        in_specs=[
            pl.BlockSpec((tm, y_pool.shape[1]), row),
            pl.BlockSpec((tm, y_lru.shape[1]), row),
            pl.BlockSpec((tm, gates.shape[1]), row),
            pl.BlockSpec((tm, d), row),
            _resident(w_pool_proj.shape), _resident(w_lru_proj.shape), _resident(w_out.shape),
        ],
        out_specs=pl.BlockSpec((tm, d), row),
        out_shape=jax.ShapeDtypeStruct((n, d), F32),
        scratch_shapes=[pltpu.VMEM((tm, d), BF16)],
        compiler_params=pltpu.CompilerParams(
            dimension_semantics=("arbitrary",),
            vmem_limit_bytes=VMEM_LIMIT_BYTES),
    )(y_pool, y_lru, gates, x, w_pool_proj, w_lru_proj, w_out)


def _ffn_kernel(apply_final, x_ref, g_ref, wg_ref, wv_ref, cw_ref, cb_ref, wd_ref, gfin_ref,
                o_ref, h2_ref, ext_ref, carry_ref):
    i = pl.program_id(1)
    f = pl.program_id(2)
    t_rows = x_ref.shape[0]

    @pl.when(f == 0)
    def _():
        x = x_ref[...]
        h2_ref[...] = _rms_norm(x, g_ref[...]).astype(BF16)
        o_ref[...] = x

    @pl.when(i == 0)
    def _():
        ext_ref[0:CONV_HALO, :] = jnp.zeros((CONV_HALO, ext_ref.shape[1]), F32)

    @pl.when(i > 0)
    def _():
        ext_ref[0:CONV_HALO, :] = carry_ref[f]

    n_taps = cw_ref.shape[0]
    n_sub = t_rows // ROW_SUB
    ups = [(_dot(h2_ref[r * ROW_SUB:(r + 1) * ROW_SUB, :], wg_ref[...]),
            _dot(h2_ref[r * ROW_SUB:(r + 1) * ROW_SUB, :], wv_ref[...])) for r in range(n_sub)]
    for r in range(n_sub):
        rows = slice(r * ROW_SUB, (r + 1) * ROW_SUB)
        base = CONV_HALO + r * ROW_SUB
        gate_pre, val = ups[r]
        ext_ref[base:base + ROW_SUB, :] = gate_pre
        conv = cb_ref[...]
        for k in range(n_taps):
            conv = conv + (ext_ref[base - k:base - k + ROW_SUB, :]
                           * cw_ref[n_taps - 1 - k:n_taps - k, :])
        act = (jax.nn.gelu(conv) * val).astype(BF16)
        o_ref[rows, :] += _dot(act, wd_ref[...])

    carry_ref[f] = ext_ref[t_rows:t_rows + CONV_HALO, :]

    if apply_final:
        @pl.when(f == pl.num_programs(2) - 1)
        def _():
            o_ref[...] = _rms_norm(o_ref[...], gfin_ref[...])


def _ffn(x, g, w_up, conv_w, conv_b, w_down, g_final, apply_final, batch, seq):
    n, d = x.shape
    d_ff = w_down.shape[0]
    t, tf = FFN_T, FFN_TF
    assert seq % t == 0 and d_ff % tf == 0
    nblk, nf = seq // t, d_ff // tf
    row = lambda b, i, f: (b * nblk + i, 0)
    return pl.pallas_call(
        functools.partial(_ffn_kernel, apply_final),
        grid=(batch, nblk, nf),
        in_specs=[
            pl.BlockSpec((t, d), row),
            pl.BlockSpec((1, d), lambda b, i, f: (0, 0)),
            pl.BlockSpec((d, tf), lambda b, i, f: (0, f)),
            pl.BlockSpec((d, tf), lambda b, i, f: (0, nf + f)),
            pl.BlockSpec((conv_w.shape[0], tf), lambda b, i, f: (0, f)),
            pl.BlockSpec((1, tf), lambda b, i, f: (0, f)),
            pl.BlockSpec((tf, d), lambda b, i, f: (f, 0)),
            pl.BlockSpec((1, d), lambda b, i, f: (0, 0)),
        ],
        out_specs=pl.BlockSpec((t, d), row),
        out_shape=jax.ShapeDtypeStruct((n, d), F32),
        scratch_shapes=[
            pltpu.VMEM((t, d), BF16),
            pltpu.VMEM((CONV_HALO + t, tf), F32),
            pltpu.VMEM((nf, CONV_HALO, tf), F32),
        ],
        compiler_params=pltpu.CompilerParams(
            dimension_semantics=("arbitrary", "arbitrary", "arbitrary"),
            vmem_limit_bytes=VMEM_LIMIT_BYTES),
    )(x, g, w_up, w_up, conv_w, conv_b, w_down, g_final)


def kernel(x, g_mix, w_in, b_gate, w_pool, pool_scale, lru_conv_w, lru_conv_b, w_a, b_a, w_i,
           b_i, lru_lambda, w_pool_proj, w_lru_proj, w_out, g_mlp, w_up, ffn_conv_w, ffn_conv_b,
           w_down, g_final):
    batch, seq, d = x.shape
    depth = w_in.shape[0]
    pool_width = pool_scale.shape[1]
    lru_width = lru_lambda.shape[1]
    mix_width = pool_width + lru_width
    row2 = lambda v: v.reshape(1, -1)
    xf = x.reshape(batch * seq, d)
    for l in range(depth):
        d_pool, v, h, *w_gate_bf, wdn_bf = _mixproj(
            xf, row2(g_mix[l]), _cast_first_columns(w_in[l], mix_width), w_in[l], lru_conv_w[l],
            row2(lru_conv_b[l]), w_down[l], pool_width, seq)
        y_pool, a, bx = _mixer(
            d_pool, v, w_pool[l].astype(BF16), row2(pool_scale[l]),
            w_a[l].astype(BF16), row2(b_a[l]), w_i[l].astype(BF16), row2(b_i[l]),
            row2(lru_lambda[l]))
        y_lru, gates, wup_bf, wpp_bf, wlp_bf, wout_bf = _gateproj(
            h, a, bx, w_gate_bf, row2(b_gate[l]),
            (w_up[l], w_pool_proj[l], w_lru_proj[l], w_out[l]), seq)
        xf = _merge(y_pool, y_lru, gates, xf, wpp_bf, wlp_bf, wout_bf)
        xf = _ffn(xf, row2(g_mlp[l]), wup_bf, ffn_conv_w[l], row2(ffn_conv_b[l]),
                  wdn_bf, row2(g_final), l == depth - 1, batch, seq)
    return xf.reshape(batch, seq, d)
```

```python
import functools

import jax
import jax.numpy as jnp
from jax import lax
from jax.experimental import pallas as pl
from jax.experimental.pallas import tpu as pltpu

F32 = jnp.float32
BF16 = jnp.bfloat16

POOL_WINDOWS = (2, 4, 8, 16)
POOL_GROUP_WIDTH = 256
LRU_BLOCK_WIDTH = 256
LRU_C = 8.0
EPS = 1e-6

SUBLANES = 8
BF16_SUBLANES = 16
POOL_SUB = 4
POOL_SUB_HALO = 16
SEQ_HALO = 24
CONV_HALO = SUBLANES
VMEM_LIMIT_BYTES = 56 * 1024 * 1024

CAST_ROWS = 256
ROW_SUB = 256
PROJ_TN = 1024
MIXPROJ_TM = 256
CONV_TN = 512
GATEPROJ_TM = 256
GATEPROJ_GROUP = 2
MERGE_TM = 512
MERGE_CHUNK = 512
FFN_T = 512
FFN_TF = 1024


def _rms_norm(x, g):
    ms = jnp.mean(x * x, axis=-1, keepdims=True)
    return x * lax.rsqrt(ms + EPS) * g


def _dot(a, b):
    return jnp.dot(a, b, preferred_element_type=F32)


def _sigmoid(x):
    return 0.5 * jnp.tanh(0.5 * x) + 0.5


def _resident(shape):
    zeros = (0,) * len(shape)
    return pl.BlockSpec(shape, lambda *_: zeros, pipeline_mode=pl.Buffered(1))


def _slab_specs(weights, n_steps, step_index):
    specs = []
    for w in weights:
        rows, cols = w.shape
        assert rows % (n_steps * BF16_SUBLANES) == 0
        specs.append(pl.BlockSpec((rows // n_steps, cols), step_index))
    return specs


def _cast_slabs(src_refs, dst_refs):
    for src, dst in zip(src_refs, dst_refs):
        dst[...] = src[...].astype(BF16)


def _cast_kernel(src_ref, dst_ref):
    dst_ref[...] = src_ref[...].astype(BF16)


def _cast_first_columns(w, width):
    rows = w.shape[0]
    assert rows % CAST_ROWS == 0 and w.shape[1] % width == 0
    return pl.pallas_call(
        _cast_kernel,
        grid=(rows // CAST_ROWS,),
        in_specs=[pl.BlockSpec((CAST_ROWS, width), lambda i: (i, 0))],
        out_specs=pl.BlockSpec((CAST_ROWS, width), lambda i: (i, 0)),
        out_shape=jax.ShapeDtypeStruct((rows, width), BF16),
    )(w)


def _mixproj_kernel(blocks_per_seq, n_cast, x_ref, g_ref, w_ref, convw_ref, convb_ref,
                    wpool_ref, pscale_ref, wa_ref, ba_ref, wi_ref, bi_ref, lam_ref, *refs):
    cast_refs = refs[:n_cast]
    ypool_ref, a_ref, bx_ref, hout_ref = refs[n_cast:n_cast + 4]
    cast_bf_refs = refs[n_cast + 4:2 * n_cast + 4]
    h_ref, ext_ref, sub_ref, carry_ref = refs[2 * n_cast + 4:]
    _cast_slabs(cast_refs, cast_bf_refs)
    tn = PROJ_TN
    n_sub = x_ref.shape[0] // ROW_SUB
    block_in_seq = lax.rem(pl.program_id(0), blocks_per_seq)

    @pl.when(block_in_seq == 0)
    def _():
        carry_ref[...] = jnp.zeros_like(carry_ref)

    @pl.loop(0, n_sub)
    def _(r):
        rows = pl.ds(pl.multiple_of(r * ROW_SUB, ROW_SUB), ROW_SUB)
        h = _rms_norm(x_ref[rows, :], g_ref[...]).astype(BF16)
        h_ref[rows, :] = h
        hout_ref[rows, :] = h

    tiles = [(0, tn)] + [(c0, CONV_TN) for c0 in range(tn, w_ref.shape[1], CONV_TN)]

    def matmuls(tile):
        c0, width = tiles[tile]
        return [_dot(h_ref[r * ROW_SUB:(r + 1) * ROW_SUB, :], w_ref[:, c0:c0 + width])
                for r in range(n_sub)]

    def place(tile, r, p):
        slot = r % 2
        width = tiles[tile][1]
        if r == 0:
            ext_ref[slot, 0:SEQ_HALO, 0:width] = carry_ref[tile, :, 0:width]
        ext_ref[slot, SEQ_HALO:SEQ_HALO + ROW_SUB, 0:width] = p
        return slot

    def pass_tail(tile, r):
        width = tiles[tile][1]
        tail = ext_ref[r % 2, ROW_SUB:ROW_SUB + SEQ_HALO, 0:width]
        if r + 1 < n_sub:
            ext_ref[(r + 1) % 2, 0:SEQ_HALO, 0:width] = tail
        else:
            carry_ref[tile, :, 0:width] = tail

    def window(load, q0, n, count, stride):
        s = load(q0, n)
        for m in range(1, count):
            s = s + load(q0 - m * stride, n)
        return s

    wide = [g for g, w in enumerate(POOL_WINDOWS) if w > POOL_SUB]
    all_ps = {tile: matmuls(tile) for tile in range(min(2, len(tiles)))}
    ps = all_ps[0]
    for r in range(n_sub):
        rows = slice(r * ROW_SUB, (r + 1) * ROW_SUB)
        slot = place(0, r, ps[r])
        pos = (block_in_seq * x_ref.shape[0] + r * ROW_SUB + 1
               + lax.broadcasted_iota(jnp.int32, (ROW_SUB, 1), 0)).astype(F32)
        for g, w in enumerate(POOL_WINDOWS):
            cols = slice(g * POOL_GROUP_WIDTH, (g + 1) * POOL_GROUP_WIDTH)
            load_ext = lambda q0, n: ext_ref[slot, q0:q0 + n, cols]
            if w <= POOL_SUB:
                s = window(load_ext, SEQ_HALO, ROW_SUB, w, 1)
            else:
                scols = slice(wide.index(g) * POOL_GROUP_WIDTH,
                              (wide.index(g) + 1) * POOL_GROUP_WIDTH)
                n = POOL_SUB_HALO + ROW_SUB
                sub_ref[slot, 0:n, scols] = window(
                    load_ext, SEQ_HALO - POOL_SUB_HALO, n, POOL_SUB, 1)
                load_sub = lambda q0, n: sub_ref[slot, q0:q0 + n, scols]
                s = window(load_sub, POOL_SUB_HALO, ROW_SUB, w // POOL_SUB, POOL_SUB)
            inv_count = 1.0 / jnp.minimum(pos, float(w))
            d = s * inv_count - ext_ref[slot, SEQ_HALO:SEQ_HALO + ROW_SUB, cols]
            y = _dot(d.astype(BF16), wpool_ref[g]) * pscale_ref[:, cols]
            ypool_ref[rows, cols] = y.astype(BF16)
        pass_tail(0, r)

    half_log_a_scale = (-0.5 * LRU_C) * jax.nn.softplus(-lam_ref[...])
    n_taps = convw_ref.shape[0]
    for tile in range(1, len(tiles)):
        c0, width = tiles[tile]
        cols = slice(c0 - tn, c0 - tn + width)
        if tile + 1 < len(tiles):
            all_ps[tile + 1] = matmuls(tile + 1)
        ps = all_ps[tile]
        for r in range(n_sub):
            slot = place(tile, r, ps[r])
            v = convb_ref[:, cols]
            for k in range(n_taps):
                v = v + (ext_ref[slot, SEQ_HALO - k:SEQ_HALO - k + ROW_SUB, 0:width]
                         * convw_ref[n_taps - 1 - k:n_taps - k, cols])
            pass_tail(tile, r)
            rows = slice(r * ROW_SUB, (r + 1) * ROW_SUB)
            for hh in range(width // LRU_BLOCK_WIDTH):
                hd = (c0 - tn) // LRU_BLOCK_WIDTH + hh
                gcols = slice(hd * LRU_BLOCK_WIDTH, (hd + 1) * LRU_BLOCK_WIDTH)
                vh = v[:, hh * LRU_BLOCK_WIDTH:(hh + 1) * LRU_BLOCK_WIDTH]
                vb = vh.astype(BF16)
                x_r = _dot(vb, wa_ref[hd]) + ba_ref[:, gcols]
                ig = _sigmoid(_dot(vb, wi_ref[hd]) + bi_ref[:, gcols])
                log_a = (jnp.tanh(0.5 * x_r) + 1.0) * half_log_a_scale[:, gcols]
                a = jnp.exp(log_a)
                q = -jnp.tanh(log_a) * (a * a + 1.0)
                mult = jnp.where(q > 0.0, q * lax.rsqrt(q), 0.0)
                a_ref[rows, gcols] = a
                bx_ref[rows, gcols] = mult * (ig * vh)


def _mixproj(x, g, w_mix, w_in, conv_w, conv_b, mixer_weights, cast_weight, pool_width, seq):
    n, d = x.shape
    mix_width = w_mix.shape[1]
    lru_width = mix_width - pool_width
    tm, tn = MIXPROJ_TM, PROJ_TN
    assert pool_width == tn == len(POOL_WINDOWS) * POOL_GROUP_WIDTH and lru_width % CONV_TN == 0
    assert n % tm == 0 and seq % tm == 0 and tm % ROW_SUB == 0
    n_wide = sum(w > POOL_SUB for w in POOL_WINDOWS)
    n_steps = n // tm
    row = lambda i: (i, 0)
    assert w_in.shape[1] % mix_width == 0 and d % (n_steps * BF16_SUBLANES) == 0
    n_gblocks = w_in.shape[1] // mix_width - 1
    gate_in_specs = [pl.BlockSpec((d // n_steps, mix_width),
                                  functools.partial(lambda k, i: (i, k), 1 + k))
                     for k in range(n_gblocks)]
    gate_out_specs = [pl.BlockSpec((d // n_steps, mix_width), row)] * n_gblocks
    return pl.pallas_call(
        functools.partial(_mixproj_kernel, seq // tm, n_gblocks + 1),
        grid=(n_steps,),
        in_specs=[
            pl.BlockSpec((tm, d), row),
            _resident((1, d)), _resident(w_mix.shape),
            _resident(conv_w.shape), _resident(conv_b.shape),
            *[_resident(w.shape) for w in mixer_weights],
            *gate_in_specs, *_slab_specs((cast_weight,), n_steps, row),
        ],
        out_specs=[pl.BlockSpec((tm, pool_width), row), pl.BlockSpec((tm, lru_width), row),
                   pl.BlockSpec((tm, lru_width), row), pl.BlockSpec((tm, d), row),
                   *gate_out_specs, *_slab_specs((cast_weight,), n_steps, row)],
        out_shape=[
            jax.ShapeDtypeStruct((n, pool_width), BF16),
            jax.ShapeDtypeStruct((n, lru_width), F32),
            jax.ShapeDtypeStruct((n, lru_width), F32),
            jax.ShapeDtypeStruct((n, d), BF16),
            *[jax.ShapeDtypeStruct((d, mix_width), BF16)] * n_gblocks,
            jax.ShapeDtypeStruct(cast_weight.shape, BF16),
        ],
        scratch_shapes=[
            pltpu.VMEM((tm, d), BF16),
            pltpu.VMEM((2, SEQ_HALO + ROW_SUB, tn), F32),
            pltpu.VMEM((2, POOL_SUB_HALO + ROW_SUB, n_wide * POOL_GROUP_WIDTH), F32),
            pltpu.VMEM((1 + lru_width // CONV_TN, SEQ_HALO, tn), F32),
        ],
        compiler_params=pltpu.CompilerParams(
            dimension_semantics=("arbitrary",),
            vmem_limit_bytes=VMEM_LIMIT_BYTES),
    )(x, g, w_mix, conv_w, conv_b, *mixer_weights, *([w_in] * n_gblocks), cast_weight)


def _gateproj_kernel(n_gelu, n_wblocks, n_cast, blocks_per_seq, h_ref, a_ref, bx_ref, *refs):
    w_refs = refs[:n_wblocks]
    b_ref = refs[n_wblocks]
    cast_refs = refs[n_wblocks + 1:n_wblocks + 1 + n_cast]
    ylru_ref, gates_ref = refs[n_wblocks + 1 + n_cast:n_wblocks + 3 + n_cast]
    cast_bf_refs = refs[n_wblocks + 3 + n_cast:n_wblocks + 3 + 2 * n_cast]
    hs_ref, hcarry_ref = refs[n_wblocks + 3 + 2 * n_cast:]
    _cast_slabs(cast_refs, cast_bf_refs)
    t_rows = h_ref.shape[0]
    tn = PROJ_TN
    tiles_per_block = w_refs[0].shape[1] // tn
    n_tiles = n_wblocks * tiles_per_block

    @pl.when(lax.rem(pl.program_id(0), blocks_per_seq) == 0)
    def _():
        hcarry_ref[...] = jnp.zeros_like(hcarry_ref)

    h = hcarry_ref[...]
    for t in range(t_rows):
        h = a_ref[t:t + 1, :] * h + bx_ref[t:t + 1, :]
        hs_ref[t:t + 1, :] = h
    hcarry_ref[...] = h

    def project(tile):
        w_ref = w_refs[tile // tiles_per_block]
        wcols = slice((tile % tiles_per_block) * tn, (tile % tiles_per_block + 1) * tn)
        return _dot(h_ref[...], w_ref[:, wcols])

    def finish(tile, p):
        if tile < n_gelu:
            cols = slice(tile * tn, (tile + 1) * tn)
            ylru_ref[:, cols] = (hs_ref[:, cols] * jax.nn.gelu(p)).astype(BF16)
        else:
            cols = slice((tile - n_gelu) * tn, (tile - n_gelu + 1) * tn)
            gates_ref[:, cols] = jax.nn.sigmoid(p + b_ref[:, cols]).astype(BF16)

    for t0 in range(0, n_tiles, GATEPROJ_GROUP):
        group = range(t0, min(t0 + GATEPROJ_GROUP, n_tiles))
        ps = [project(tile) for tile in group]
        for tile, p in zip(group, ps):
            finish(tile, p)


def _gateproj(h, a, bx, w_blocks, b_gate, cast_weights, seq):
    n, d = h.shape
    gelu_width = a.shape[1]
    n_wblocks = len(w_blocks)
    gate_width = sum(w.shape[1] for w in w_blocks) - gelu_width
    tm, tn = GATEPROJ_TM, PROJ_TN
    assert n % tm == 0 and seq % tm == 0 and gelu_width % tn == 0 and gate_width % tn == 0
    assert all(w.shape == w_blocks[0].shape for w in w_blocks) and w_blocks[0].shape[1] % tn == 0
    row = lambda i: (i, 0)
    w_specs = [_resident(w.shape) for w in w_blocks]
    return pl.pallas_call(
        functools.partial(_gateproj_kernel, gelu_width // tn, n_wblocks, len(cast_weights),
                          seq // tm),
        grid=(n // tm,),
        in_specs=[pl.BlockSpec((tm, d), row), pl.BlockSpec((tm, gelu_width), row),
                  pl.BlockSpec((tm, gelu_width), row), *w_specs, _resident(b_gate.shape),
                  *_slab_specs(cast_weights, n // tm, row)],
        out_specs=[pl.BlockSpec((tm, gelu_width), row), pl.BlockSpec((tm, gate_width), row),
                   *_slab_specs(cast_weights, n // tm, row)],
        out_shape=[jax.ShapeDtypeStruct((n, gelu_width), BF16),
                   jax.ShapeDtypeStruct((n, gate_width), BF16),
                   *[jax.ShapeDtypeStruct(w.shape, BF16) for w in cast_weights]],
        scratch_shapes=[
            pltpu.VMEM((tm, gelu_width), F32),
            pltpu.VMEM((1, gelu_width), F32),
        ],
        compiler_params=pltpu.CompilerParams(
            dimension_semantics=("arbitrary",),
            vmem_limit_bytes=VMEM_LIMIT_BYTES),
    )(h, a, bx, *w_blocks, b_gate, *cast_weights)


def _merge_kernel(yp_ref, yl_ref, gates_ref, x_ref, wpp_ref, wlp_ref, wout_ref, o_ref, m_ref):
    d = o_ref.shape[1]
    ch = MERGE_CHUNK
    for c in range(d // ch):
        cols = slice(c * ch, (c + 1) * ch)
        p = _dot(yp_ref[...], wpp_ref[:, cols])
        q = _dot(yl_ref[...], wlp_ref[:, cols])
        g_pool = gates_ref[:, cols].astype(F32)
        g_lru = gates_ref[:, d + c * ch:d + (c + 1) * ch].astype(F32)
        m_ref[:, cols] = (g_pool * p + g_lru * q).astype(BF16)
    for c in range(d // ch):
        cols = slice(c * ch, (c + 1) * ch)
        o_ref[:, cols] = x_ref[:, cols] + _dot(m_ref[...], wout_ref[:, cols])


def _merge(y_pool, y_lru, gates, x, w_pool_proj, w_lru_proj, w_out):
    n, d = x.shape
    tm = MERGE_TM
    assert n % tm == 0 and d % MERGE_CHUNK == 0
    row = lambda i: (i, 0)
    return pl.pallas_call(
        _merge_kernel,
        grid=(n // tm,),
        in_specs=[
            pl.BlockSpec((tm, y_pool.shape[1]), row),
            pl.BlockSpec((tm, y_lru.shape[1]), row),
            pl.BlockSpec((tm, gates.shape[1]), row),
            pl.BlockSpec((tm, d), row),
            _resident(w_pool_proj.shape), _resident(w_lru_proj.shape), _resident(w_out.shape),
        ],
        out_specs=pl.BlockSpec((tm, d), row),
        out_shape=jax.ShapeDtypeStruct((n, d), F32),
        scratch_shapes=[pltpu.VMEM((tm, d), BF16)],
        compiler_params=pltpu.CompilerParams(
            dimension_semantics=("arbitrary",),
            vmem_limit_bytes=VMEM_LIMIT_BYTES),
    )(y_pool, y_lru, gates, x, w_pool_proj, w_lru_proj, w_out)


def _ffn_kernel(apply_final, x_ref, g_ref, wg_ref, wv_ref, cw_ref, cb_ref, wd_ref, gfin_ref,
                o_ref, h2_ref, ext_ref, carry_ref):
    i = pl.program_id(1)
    f = pl.program_id(2)
    t_rows = x_ref.shape[0]

    @pl.when(f == 0)
    def _():
        x = x_ref[...]
        h2_ref[...] = _rms_norm(x, g_ref[...]).astype(BF16)
        o_ref[...] = x

    @pl.when(i == 0)
    def _():
        ext_ref[0:CONV_HALO, :] = jnp.zeros((CONV_HALO, ext_ref.shape[1]), F32)

    @pl.when(i > 0)
    def _():
        ext_ref[0:CONV_HALO, :] = carry_ref[f]

    n_taps = cw_ref.shape[0]
    n_sub = t_rows // ROW_SUB
    ups = [(_dot(h2_ref[r * ROW_SUB:(r + 1) * ROW_SUB, :], wg_ref[...]),
            _dot(h2_ref[r * ROW_SUB:(r + 1) * ROW_SUB, :], wv_ref[...])) for r in range(n_sub)]
    for r in range(n_sub):
        rows = slice(r * ROW_SUB, (r + 1) * ROW_SUB)
        base = CONV_HALO + r * ROW_SUB
        gate_pre, val = ups[r]
        ext_ref[base:base + ROW_SUB, :] = gate_pre
        conv = cb_ref[...]
        for k in range(n_taps):
            conv = conv + (ext_ref[base - k:base - k + ROW_SUB, :]
                           * cw_ref[n_taps - 1 - k:n_taps - k, :])
        act = (jax.nn.gelu(conv) * val).astype(BF16)
        o_ref[rows, :] += _dot(act, wd_ref[...])

    carry_ref[f] = ext_ref[t_rows:t_rows + CONV_HALO, :]

    if apply_final:
        @pl.when(f == pl.num_programs(2) - 1)
        def _():
            o_ref[...] = _rms_norm(o_ref[...], gfin_ref[...])


def _ffn(x, g, w_up, conv_w, conv_b, w_down, g_final, apply_final, batch, seq):
    n, d = x.shape
    d_ff = w_down.shape[0]
    t, tf = FFN_T, FFN_TF
    assert seq % t == 0 and d_ff % tf == 0
    nblk, nf = seq // t, d_ff // tf
    row = lambda b, i, f: (b * nblk + i, 0)
    return pl.pallas_call(
        functools.partial(_ffn_kernel, apply_final),
        grid=(batch, nblk, nf),
        in_specs=[
            pl.BlockSpec((t, d), row),
            pl.BlockSpec((1, d), lambda b, i, f: (0, 0)),
            pl.BlockSpec((d, tf), lambda b, i, f: (0, f)),
            pl.BlockSpec((d, tf), lambda b, i, f: (0, nf + f)),
            pl.BlockSpec((conv_w.shape[0], tf), lambda b, i, f: (0, f)),
            pl.BlockSpec((1, tf), lambda b, i, f: (0, f)),
            pl.BlockSpec((tf, d), lambda b, i, f: (f, 0)),
            pl.BlockSpec((1, d), lambda b, i, f: (0, 0)),
        ],
        out_specs=pl.BlockSpec((t, d), row),
        out_shape=jax.ShapeDtypeStruct((n, d), F32),
        scratch_shapes=[
            pltpu.VMEM((t, d), BF16),
            pltpu.VMEM((CONV_HALO + t, tf), F32),
            pltpu.VMEM((nf, CONV_HALO, tf), F32),
        ],
        compiler_params=pltpu.CompilerParams(
            dimension_semantics=("arbitrary", "arbitrary", "arbitrary"),
            vmem_limit_bytes=VMEM_LIMIT_BYTES),
    )(x, g, w_up, w_up, conv_w, conv_b, w_down, g_final)


def kernel(x, g_mix, w_in, b_gate, w_pool, pool_scale, lru_conv_w, lru_conv_b, w_a, b_a, w_i,
           b_i, lru_lambda, w_pool_proj, w_lru_proj, w_out, g_mlp, w_up, ffn_conv_w, ffn_conv_b,
           w_down, g_final):
    batch, seq, d = x.shape
    depth = w_in.shape[0]
    pool_width = pool_scale.shape[1]
    lru_width = lru_lambda.shape[1]
    mix_width = pool_width + lru_width
    row2 = lambda v: v.reshape(1, -1)
    xf = x.reshape(batch * seq, d)
    for l in range(depth):
        mixer_weights = (w_pool[l].astype(BF16), row2(pool_scale[l]), w_a[l].astype(BF16),
                         row2(b_a[l]), w_i[l].astype(BF16), row2(b_i[l]), row2(lru_lambda[l]))
        y_pool, a, bx, h, *w_gate_bf, wdn_bf = _mixproj(
            xf, row2(g_mix[l]), _cast_first_columns(w_in[l], mix_width), w_in[l], lru_conv_w[l],
            row2(lru_conv_b[l]), mixer_weights, w_down[l], pool_width, seq)
        y_lru, gates, wup_bf, wpp_bf, wlp_bf, wout_bf = _gateproj(
            h, a, bx, w_gate_bf, row2(b_gate[l]),
            (w_up[l], w_pool_proj[l], w_lru_proj[l], w_out[l]), seq)
        xf = _merge(y_pool, y_lru, gates, xf, wpp_bf, wlp_bf, wout_bf)
        xf = _ffn(xf, row2(g_mlp[l]), wup_bf, ffn_conv_w[l], row2(ffn_conv_b[l]),
                  wdn_bf, row2(g_final), l == depth - 1, batch, seq)
    return xf.reshape(batch, seq, d)
```
